```python
import jax, jax.numpy as jnp
from jax import lax
import numpy as np

D_MODEL = 1024
BATCH = 4
SEQ = 4096
DEPTH = 2

MEM_LEN = 256
EPS = 1e-6
GM_GROUPS = 8
GM_GROUP_DIM = D_MODEL // GM_GROUPS
GM_WIDTH = GM_GROUPS * GM_GROUP_DIM
GM_CHUNK = 128
SSD_INNER = 2 * D_MODEL
SSD_HEAD_DIM = 64
SSD_HEADS = SSD_INNER // SSD_HEAD_DIM
SSD_GROUPS = 4
SSD_HPG = SSD_HEADS // SSD_GROUPS
SSD_STATE = 128
SSD_CONV = 4
SSD_CHUNK = 128
SSD_XBC = SSD_INNER + 2 * SSD_GROUPS * SSD_STATE
X_HEADS = 4
X_HEAD_DIM = D_MODEL // X_HEADS
FFN_DIM = 2816
FFN_CONV = 3
N_BRANCH = 2
IN_SIZES = (GM_WIDTH, GM_WIDTH, SSD_INNER, SSD_XBC, SSD_HEADS, D_MODEL, D_MODEL)
IN_COLS = sum(IN_SIZES)

kernel_name = "hybrid_gmlp_ssd_memxattn_convffn"


def rmsnorm(x, g):
    xf = x.astype(jnp.float32)
    y = xf * lax.rsqrt(jnp.mean(xf * xf, axis=-1, keepdims=True) + EPS)
    return (y * g.astype(jnp.float32)).astype(x.dtype)


def causal_dwconv(x, w, b):
    k_width = w.shape[0]
    s = x.shape[1]
    xp = jnp.pad(x, ((0, 0), (k_width - 1, 0), (0, 0)))
    y = b
    for k in range(k_width):
        y = y + xp[:, k:k + s, :] * w[k]
    return y


def gmlp_branch(u, v, v_gain, w_s, b_s):
    bsz, s, _ = v.shape
    n_chunks = s // GM_CHUNK
    u = jax.nn.gelu(u)
    v = rmsnorm(jax.nn.gelu(v), v_gain)
    vc = v.reshape(bsz, n_chunks, GM_CHUNK, GM_GROUPS, GM_GROUP_DIM)
    mask = jnp.tril(jnp.ones((GM_CHUNK, GM_CHUNK), dtype=w_s.dtype))
    ws = w_s * mask
    mixed = jnp.einsum('gts,bcsgd->bctgd', ws, vc) + b_s.T[None, None, :, :, None]
    return u * mixed.reshape(bsz, s, GM_WIDTH)


def ssd_branch(z, xbc, dt_raw, conv_w, conv_b, dt_bias, a_log, d_skip, norm_g):
    bsz, s, _ = z.shape
    nc, L = s // SSD_CHUNK, SSD_CHUNK
    G, R, P, N = SSD_GROUPS, SSD_HPG, SSD_HEAD_DIM, SSD_STATE
    xbc = jax.nn.silu(causal_dwconv(xbc, conv_w, conv_b))
    xs, bm, cm = jnp.split(xbc, [SSD_INNER, SSD_INNER + G * N], axis=-1)
    dt = jax.nn.softplus(dt_raw.astype(jnp.float32) + dt_bias.astype(jnp.float32))
    a = -jnp.exp(a_log.astype(jnp.float32))
    x = xs.reshape(bsz, nc, L, G, R, P)
    bc = bm.reshape(bsz, nc, L, G, N)
    cc = cm.reshape(bsz, nc, L, G, N)
    dt_c = dt.reshape(bsz, nc, L, G, R)
    xd = x * dt_c[..., None]
    a_cs = jnp.cumsum(jnp.moveaxis((dt * a).reshape(bsz, nc, L, G, R), 2, -1), axis=-1)
    causal = jnp.tril(jnp.ones((L, L), dtype=bool))
    seg = a_cs[..., :, None] - a_cs[..., None, :]
    lmat = jnp.exp(jnp.where(causal, seg, -jnp.inf))
    cb = jnp.einsum('bclgn,bcsgn->bcgls', cc, bc)
    scores = cb[:, :, :, None] * lmat
    y_diag = jnp.einsum('bcgrls,bcsgrp->bclgrp', scores, xd)
    decay = jnp.exp(a_cs[..., -1:] - a_cs)
    states = jnp.einsum('bcsgn,bcgrs,bcsgrp->bcgrpn', bc, decay, xd)
    chunk_decay = jnp.exp(a_cs[..., -1])

    def step(h, inp):
        st, dec = inp
        return h * dec[..., None, None] + st, h

    h0 = jnp.zeros_like(states[:, 0])
    _, prev = lax.scan(step, h0, (jnp.moveaxis(states, 1, 0), jnp.moveaxis(chunk_decay, 1, 0)))
    prev = jnp.moveaxis(prev, 0, 1)
    y_off = jnp.einsum('bclgn,bcgrpn->bclgrp', cc, prev) * jnp.exp(jnp.moveaxis(a_cs, -1, 2))[..., None]
    y = y_diag + y_off + x * d_skip.reshape(G, R)[:, :, None]
    y = y.reshape(bsz, s, SSD_INNER) * jax.nn.silu(z)
    yg = y.reshape(bsz, s, G, SSD_INNER // G)
    yf = yg.astype(jnp.float32)
    yf = yf * lax.rsqrt(jnp.mean(yf * yf, axis=-1, keepdims=True) + EPS)
    return (yf.reshape(bsz, s, SSD_INNER) * norm_g.astype(jnp.float32)).astype(y.dtype)


def mem_cross_attn(h, mem_n, w_q, w_kv, w_o):
    bsz, s, _ = h.shape
    m = mem_n.shape[1]
    q = (h @ w_q).reshape(bsz, s, X_HEADS, X_HEAD_DIM)
    k, v = jnp.split(mem_n @ w_kv, 2, axis=-1)
    k = k.reshape(bsz, m, X_HEADS, X_HEAD_DIM)
    v = v.reshape(bsz, m, X_HEADS, X_HEAD_DIM)
    sc = jnp.einsum('bshd,bmhd->bhsm', q, k).astype(jnp.float32) * (X_HEAD_DIM ** -0.5)
    p = jax.nn.softmax(sc, axis=-1).astype(v.dtype)
    o = jnp.einsum('bhsm,bmhd->bshd', p, v).reshape(bsz, s, D_MODEL)
    return o @ w_o


def conv_ffn(h, w_up, conv_w, conv_b, w_down):
    up = causal_dwconv(h @ w_up, conv_w, conv_b)
    a, b = jnp.split(up, 2, axis=-1)
    return (jax.nn.gelu(a) * b) @ w_down


def setup_inputs(seed: int = 0) -> dict:
    key = jax.random.key(seed)
    ks = iter(jax.random.split(key, 40))

    def nrm(shape, scale):
        return jax.random.normal(next(ks), shape, jnp.float32) * scale

    def gain(width):
        return 1.0 + nrm((DEPTH, width), 0.02)

    dt0 = jnp.exp(jax.random.uniform(next(ks), (DEPTH, SSD_HEADS), jnp.float32,
                                     np.log(1e-3), np.log(1e-1)))
    dt_bias = dt0 + jnp.log(-jnp.expm1(-dt0))
    a_log = jnp.log(jax.random.uniform(next(ks), (DEPTH, SSD_HEADS), jnp.float32, 1.0, 16.0))
    return {
        "x": nrm((BATCH, SEQ, D_MODEL), 1.0),
        "mem": nrm((BATCH, MEM_LEN, D_MODEL), 1.0),
        "norm_pre_mix": gain(D_MODEL),
        "norm_post_mix": gain(D_MODEL),
        "norm_pre_mem": gain(D_MODEL),
        "norm_mem_kv": gain(D_MODEL),
        "norm_post_mem": gain(D_MODEL),
        "norm_pre_ffn": gain(D_MODEL),
        "norm_post_ffn": gain(D_MODEL),
        "w_in": nrm((DEPTH, D_MODEL, IN_COLS), D_MODEL ** -0.5),
        "gm_v_norm": gain(GM_WIDTH),
        "gm_w_s": nrm((DEPTH, GM_GROUPS, GM_CHUNK, GM_CHUNK), GM_CHUNK ** -0.5),
        "gm_b_s": 1.0 + nrm((DEPTH, GM_GROUPS, GM_CHUNK), 0.02),
        "ssd_conv_w": nrm((DEPTH, SSD_CONV, SSD_XBC), SSD_CONV ** -0.5),
        "ssd_conv_b": nrm((DEPTH, SSD_XBC), 0.02),
        "ssd_dt_bias": dt_bias,
        "ssd_a_log": a_log,
        "ssd_d": 1.0 + nrm((DEPTH, SSD_HEADS), 0.02),
        "ssd_norm": gain(SSD_INNER),
        "w_branch_a": nrm((DEPTH, GM_WIDTH, D_MODEL), GM_WIDTH ** -0.5),
        "w_branch_b": nrm((DEPTH, SSD_INNER, D_MODEL), SSD_INNER ** -0.5),
        "w_out": nrm((DEPTH, D_MODEL, D_MODEL), D_MODEL ** -0.5),
        "xa_w_q": nrm((DEPTH, D_MODEL, D_MODEL), D_MODEL ** -0.5),
        "xa_w_kv": nrm((DEPTH, D_MODEL, 2 * D_MODEL), D_MODEL ** -0.5),
        "xa_w_o": nrm((DEPTH, D_MODEL, D_MODEL), D_MODEL ** -0.5),
        "ffn_w_up": nrm((DEPTH, D_MODEL, 2 * FFN_DIM), D_MODEL ** -0.5),
        "ffn_conv_w": nrm((DEPTH, FFN_CONV, 2 * FFN_DIM), FFN_CONV ** -0.5),
        "ffn_conv_b": nrm((DEPTH, 2 * FFN_DIM), 0.02),
        "ffn_w_down": nrm((DEPTH, FFN_DIM, D_MODEL), FFN_DIM ** -0.5),
    }


def reference(x, mem, norm_pre_mix, norm_post_mix, norm_pre_mem, norm_mem_kv, norm_post_mem,
              norm_pre_ffn, norm_post_ffn, w_in, gm_v_norm, gm_w_s, gm_b_s, ssd_conv_w, ssd_conv_b,
              ssd_dt_bias, ssd_a_log, ssd_d, ssd_norm, w_branch_a, w_branch_b, w_out,
              xa_w_q, xa_w_kv, xa_w_o, ffn_w_up, ffn_conv_w, ffn_conv_b, ffn_w_down):
    split_points = [int(i) for i in np.cumsum(IN_SIZES)[:-1]]
    for l in range(DEPTH):
        h = rmsnorm(x, norm_pre_mix[l])
        proj = h @ w_in[l]
        u, v, z, xbc, dt_raw, g_a, g_b = jnp.split(proj, split_points, axis=-1)
        y_a = gmlp_branch(u, v, gm_v_norm[l], gm_w_s[l], gm_b_s[l]) @ w_branch_a[l]
        y_b = ssd_branch(z, xbc, dt_raw, ssd_conv_w[l], ssd_conv_b[l], ssd_dt_bias[l],
                         ssd_a_log[l], ssd_d[l], ssd_norm[l]) @ w_branch_b[l]
        mixed = jax.nn.sigmoid(g_a) * y_a + jax.nn.sigmoid(g_b) * y_b
        x = x + rmsnorm(mixed @ w_out[l], norm_post_mix[l])
        h = rmsnorm(x, norm_pre_mem[l])
        mem_n = rmsnorm(mem, norm_mem_kv[l])
        x = x + rmsnorm(mem_cross_attn(h, mem_n, xa_w_q[l], xa_w_kv[l], xa_w_o[l]), norm_post_mem[l])
        h = rmsnorm(x, norm_pre_ffn[l])
        x = x + rmsnorm(conv_ffn(h, ffn_w_up[l], ffn_conv_w[l], ffn_conv_b[l], ffn_w_down[l]),
                        norm_post_ffn[l])
    return x
```

```python
import functools

import jax
import jax.numpy as jnp
from jax import lax
from jax.experimental import pallas as pl
from jax.experimental.pallas import tpu as pltpu

F32 = jnp.float32
BF16 = jnp.bfloat16

D_MODEL = 1024
MEM_LEN = 256
EPS = 1e-6
GM_GROUPS = 8
GM_WIDTH = 1024
CHUNK = 128
SSD_INNER = 2048
SSD_HEAD_DIM = 64
SSD_HEADS = 32
SSD_GROUPS = 4
SSD_HPG = 8
SSD_STATE = 128
SSD_CONV = 4
SSD_BC = 2 * SSD_GROUPS * SSD_STATE
GROUP_COLS = SSD_INNER // SSD_GROUPS
X_HEADS = 4
X_HEAD_DIM = 256
FFN_DIM = 2816
FFN_CONV = 3
LANES = 128
TAIL_ROWS = 8
MAIN_COLS = 9216
DT_OFF = 7168

VMEM_LIMIT = 56 * 1024 * 1024


def _params(*sem):
    return pltpu.CompilerParams(dimension_semantics=sem, vmem_limit_bytes=VMEM_LIMIT)


def _dot(a, b):
    return jnp.dot(a, b, preferred_element_type=F32)


def _dot_nt(a, b):
    return lax.dot_general(a, b, (((1,), (1,)), ((), ())), preferred_element_type=F32)


def _rms(x, g):
    ms = jnp.mean(x * x, axis=-1, keepdims=True)
    return x * lax.rsqrt(ms + EPS) * g


def _gelu(x):
    return jax.nn.gelu(x, approximate=True)


def _sigmoid(x):
    return 1.0 / (1.0 + jnp.exp(-x))


def _silu(x):
    return x * _sigmoid(x)


def _softplus(x):
    return jnp.maximum(x, 0.0) + jnp.log1p(jnp.exp(-jnp.abs(x)))


def _causal_conv(cur, tail, w, b):
    k_width = w.shape[0]
    row = lax.broadcasted_iota(jnp.int32, tail.shape, 0)
    y = b + w[k_width - 1:k_width, :] * cur
    for s in range(1, k_width):
        r = pltpu.roll(cur, s, axis=0)
        rt = pltpu.roll(tail, s, axis=0)
        head = jnp.where(row < s, rt, r[0:TAIL_ROWS, :])
        r = jnp.concatenate([head, r[TAIL_ROWS:, :]], axis=0)
        y = y + w[k_width - 1 - s:k_width - s, :] * r
    return y


def _inproj_kernel(x_ref, g_ref, w_ref, wdt_ref, o_ref, dt_ref, h_scr):
    @pl.when(pl.program_id(1) == 0)
    def _():
        h = _rms(x_ref[...], g_ref[...]).astype(BF16)
        h_scr[...] = h
        dt_ref[...] = _dot(h, wdt_ref[...])

    o_ref[...] = _dot(h_scr[...], w_ref[...]).astype(BF16)


def _inproj(x, g, w_main, w_dt, tm=1024, tn=1024):
    m = x.shape[0]
    n = w_main.shape[1]
    return pl.pallas_call(
        _inproj_kernel,
        grid=(m // tm, n // tn),
        in_specs=[
            pl.BlockSpec((tm, D_MODEL), lambda i, j: (i, 0)),
            pl.BlockSpec((1, D_MODEL), lambda i, j: (0, 0)),
            pl.BlockSpec((D_MODEL, tn), lambda i, j: (0, j)),
            pl.BlockSpec((D_MODEL, LANES), lambda i, j: (0, 0)),
        ],
        out_specs=[
            pl.BlockSpec((tm, tn), lambda i, j: (i, j)),
            pl.BlockSpec((tm, LANES), lambda i, j: (i, 0)),
        ],
        out_shape=[
            jax.ShapeDtypeStruct((m, n), BF16),
            jax.ShapeDtypeStruct((m, LANES), F32),
        ],
        scratch_shapes=[pltpu.VMEM((tm, D_MODEL), BF16)],
        compiler_params=_params("arbitrary", "arbitrary"),
        name="inproj",
    )(x, g, w_main, w_dt)


def _kvproj_kernel(x_ref, g_ref, w_ref, o_ref):
    h = _rms(x_ref[...], g_ref[...]).astype(BF16)
    o_ref[...] = _dot(h, w_ref[...]).astype(BF16)


def _kvproj(mem, g, w_kv, tm=512):
    m = mem.shape[0]
    n = w_kv.shape[1]
    return pl.pallas_call(
        _kvproj_kernel,
        grid=(m // tm,),
        in_specs=[
            pl.BlockSpec((tm, D_MODEL), lambda i: (i, 0)),
            pl.BlockSpec((1, D_MODEL), lambda i: (0, 0)),
            pl.BlockSpec((D_MODEL, n), lambda i: (0, 0)),
        ],
        out_specs=pl.BlockSpec((tm, n), lambda i: (i, 0)),
        out_shape=jax.ShapeDtypeStruct((m, n), BF16),
        compiler_params=_params("arbitrary"),
        name="kvproj",
    )(mem, g, w_kv)


def _split2(x):
    hi = x.astype(BF16)
    lo = (x - hi.astype(F32)).astype(BF16)
    return hi, lo


def _split3(x):
    hi = x.astype(BF16)
    r = x - hi.astype(F32)
    mid = r.astype(BF16)
    lo = (r - mid.astype(F32)).astype(BF16)
    return hi, mid, lo


def _mixer_kernel(u_ref, v_ref, z_ref, xs_ref, bc_ref, dt_ref,
                  vgain_ref, ws_ref, bexp_ref,
                  cwx_ref, cbx_ref, cwbc_ref, cbbc_ref,
                  dtb_ref, alog_ref, dexp_ref, ng_ref, e_ref,
                  ya_ref, yb_ref,
                  state_scr, tailx_scr, tailbc_scr, wsm_scr):
    c = pl.program_id(1)
    L = CHUNK
    rowi = lax.broadcasted_iota(jnp.int32, (L, L), 0)
    coli = lax.broadcasted_iota(jnp.int32, (L, L), 1)
    causal = rowi >= coli

    @pl.when(c == 0)
    def _():
        state_scr[...] = jnp.zeros_like(state_scr)
        tailx_scr[...] = jnp.zeros_like(tailx_scr)
        tailbc_scr[...] = jnp.zeros_like(tailbc_scr)
        for g in range(GM_GROUPS):
            wsm_scr[g] = jnp.where(causal, ws_ref[g], 0.0).astype(BF16)

    vn = _rms(_gelu(v_ref[...].astype(F32)), vgain_ref[...]).astype(BF16)
    for g in range(GM_GROUPS):
        sl = slice(g * LANES, (g + 1) * LANES)
        mixed = _dot(wsm_scr[g], vn[:, sl]) + bexp_ref[:, sl]
        ya_ref[:, sl] = (_gelu(u_ref[:, sl].astype(F32)) * mixed).astype(BF16)

    xs_raw = xs_ref[...].astype(F32)
    bc_raw = bc_ref[...].astype(F32)
    xs = _silu(_causal_conv(xs_raw, tailx_scr[...], cwx_ref[...], cbx_ref[...]))
    bc = _silu(_causal_conv(bc_raw, tailbc_scr[...], cwbc_ref[...], cbbc_ref[...]))
    tailx_scr[...] = xs_raw[L - TAIL_ROWS:, :]
    tailbc_scr[...] = bc_raw[L - TAIL_ROWS:, :]

    lane = lax.broadcasted_iota(jnp.int32, (L, LANES), 1)
    head_lane = lane < SSD_HEADS
    dt = jnp.where(head_lane, _softplus(dt_ref[...] + dtb_ref[...]), 0.0)
    a_row = -jnp.exp(alog_ref[...])
    da = dt * a_row
    tril = jnp.where(causal, 1.0, 0.0).astype(BF16)
    d_hi, d_mid, d_lo = _split3(da)
    a_cs = _dot(tril, d_hi) + _dot(tril, d_mid) + _dot(tril, d_lo)
    a_cs_t = a_cs.T
    dt_t = dt.T
    ea = jnp.exp(a_cs)
    w_st = dt * jnp.exp(a_cs[L - 1:L, :] - a_cs)
    e_mat = e_ref[...]
    ea_hi, ea_lo = _split2(ea)
    ea_exp = _dot(ea_hi, e_mat) + _dot(ea_lo, e_mat)
    ws_hi, ws_lo = _split2(w_st)
    wst_exp = _dot(ws_hi, e_mat) + _dot(ws_lo, e_mat)

    lane_pair = lax.broadcasted_iota(jnp.int32, (L, LANES), 1)
    first_half = lane_pair < SSD_HEAD_DIM
    zero_bf = jnp.zeros((L, LANES), BF16)

    for g in range(SSD_GROUPS):
        gsl = slice(g * GROUP_COLS, (g + 1) * GROUP_COLS)
        bm_g = bc[:, g * SSD_STATE:(g + 1) * SSD_STATE]
        cm_g = bc[:, SSD_GROUPS * SSD_STATE + g * SSD_STATE:
                  SSD_GROUPS * SSD_STATE + (g + 1) * SSD_STATE]
        bm_bf = bm_g.astype(BF16)
        cm_bf = cm_g.astype(BF16)
        cb = _dot_nt(cm_bf, bm_bf)
        xs_g = xs[:, gsl]
        xs_g_bf = xs_g.astype(BF16)
        state_g = state_scr[:, gsl]
        y_g = _dot(cm_bf, state_g.astype(BF16)) * ea_exp[:, gsl]
        parts = []
        for pr in range(SSD_HPG // 2):
            sc = []
            for hh in range(2):
                h = g * SSD_HPG + 2 * pr + hh
                seg = a_cs[:, h:h + 1] - a_cs_t[h:h + 1, :]
                lm = jnp.exp(jnp.where(causal, seg, -jnp.inf))
                sc.append((cb * lm * dt_t[h:h + 1, :]).astype(BF16))
            lhs = jnp.concatenate(sc, axis=1)
            x_pair = xs_g_bf[:, pr * LANES:(pr + 1) * LANES]
            rhs = jnp.concatenate([jnp.where(first_half, x_pair, zero_bf),
                                   jnp.where(first_half, zero_bf, x_pair)], axis=0)
            parts.append(_dot(lhs, rhs))
        y_g = y_g + jnp.concatenate(parts, axis=1) + xs_g * dexp_ref[:, gsl]
        xw = (xs_g * wst_exp[:, gsl]).astype(BF16)
        new_state = state_g * ea_exp[L - 1:L, gsl] + _dot(bm_g.T.astype(BF16), xw)
        state_scr[:, gsl] = new_state
        y_g = y_g * _silu(z_ref[:, gsl].astype(F32))
        yb_ref[:, gsl] = _rms(y_g, ng_ref[:, gsl]).astype(BF16)


def _mixer(proj, dt, layer_w, batch, seq):
    nc = seq // CHUNK
    row = lambda b, c: b * nc + c
    const2 = lambda b, c: (0, 0)
    in_specs = [
        pl.BlockSpec((CHUNK, GM_WIDTH), lambda b, c: (row(b, c), 0)),
        pl.BlockSpec((CHUNK, GM_WIDTH), lambda b, c: (row(b, c), 1)),
        pl.BlockSpec((CHUNK, SSD_INNER), lambda b, c: (row(b, c), 1)),
        pl.BlockSpec((CHUNK, SSD_INNER), lambda b, c: (row(b, c), 2)),
        pl.BlockSpec((CHUNK, SSD_BC), lambda b, c: (row(b, c), 6)),
        pl.BlockSpec((CHUNK, LANES), lambda b, c: (row(b, c), 0)),
        pl.BlockSpec((1, GM_WIDTH), const2),
        pl.BlockSpec((GM_GROUPS, CHUNK, CHUNK), lambda b, c: (0, 0, 0)),
        pl.BlockSpec((CHUNK, GM_WIDTH), const2),
        pl.BlockSpec((SSD_CONV, SSD_INNER), const2),
        pl.BlockSpec((1, SSD_INNER), const2),
        pl.BlockSpec((SSD_CONV, SSD_BC), const2),
        pl.BlockSpec((1, SSD_BC), const2),
        pl.BlockSpec((1, LANES), const2),
        pl.BlockSpec((1, LANES), const2),
        pl.BlockSpec((1, SSD_INNER), const2),
        pl.BlockSpec((1, SSD_INNER), const2),
        pl.BlockSpec((LANES, SSD_INNER), const2),
    ]
    m = batch * seq
    return pl.pallas_call(
        _mixer_kernel,
        grid=(batch, nc),
        in_specs=in_specs,
        out_specs=[
            pl.BlockSpec((CHUNK, GM_WIDTH), lambda b, c: (row(b, c), 0)),
            pl.BlockSpec((CHUNK, SSD_INNER), lambda b, c: (row(b, c), 0)),
        ],
        out_shape=[
            jax.ShapeDtypeStruct((m, GM_WIDTH), BF16),
            jax.ShapeDtypeStruct((m, SSD_INNER), BF16),
        ],
        scratch_shapes=[
            pltpu.VMEM((SSD_STATE, SSD_INNER), F32),
            pltpu.VMEM((TAIL_ROWS, SSD_INNER), F32),
            pltpu.VMEM((TAIL_ROWS, SSD_BC), F32),
            pltpu.VMEM((GM_GROUPS, CHUNK, CHUNK), BF16),
        ],
        compiler_params=_params("arbitrary", "arbitrary"),
        name="mixer",
    )(proj, proj, proj, proj, proj, dt, *layer_w)


def _merge_kernel(ya_ref, yb_ref, ga_ref, gb_ref, x_ref, wa_ref, wb_ref, wo_ref, g_ref, o_ref):
    ya = _dot(ya_ref[...], wa_ref[...])
    yb = _dot(yb_ref[...], wb_ref[...])
    mixed = _sigmoid(ga_ref[...].astype(F32)) * ya + _sigmoid(gb_ref[...].astype(F32)) * yb
    y = _dot(mixed.astype(BF16), wo_ref[...])
    o_ref[...] = x_ref[...] + _rms(y, g_ref[...])


def _merge(ya, yb, proj, x, w_a, w_b, w_o, g, tm=512):
    m = x.shape[0]
    const = lambda i: (0, 0)
    return pl.pallas_call(
        _merge_kernel,
        grid=(m // tm,),
        in_specs=[
            pl.BlockSpec((tm, GM_WIDTH), lambda i: (i, 0)),
            pl.BlockSpec((tm, SSD_INNER), lambda i: (i, 0)),
            pl.BlockSpec((tm, D_MODEL), lambda i: (i, 7)),
            pl.BlockSpec((tm, D_MODEL), lambda i: (i, 8)),
            pl.BlockSpec((tm, D_MODEL), lambda i: (i, 0)),
            pl.BlockSpec((GM_WIDTH, D_MODEL), const),
            pl.BlockSpec((SSD_INNER, D_MODEL), const),
            pl.BlockSpec((D_MODEL, D_MODEL), const),
            pl.BlockSpec((1, D_MODEL), const),
        ],
        out_specs=pl.BlockSpec((tm, D_MODEL), lambda i: (i, 0)),
        out_shape=jax.ShapeDtypeStruct((m, D_MODEL), F32),
        compiler_params=_params("arbitrary"),
        name="merge",
    )(ya, yb, proj, proj, x, w_a, w_b, w_o, g)


def _xattn_kernel(x_ref, gpre_ref, wq_ref, k_ref, v_ref, wo_ref, gpost_ref, o_ref):
    x = x_ref[...]
    h = _rms(x, gpre_ref[...]).astype(BF16)
    q = (_dot(h, wq_ref[...]) * (X_HEAD_DIM ** -0.5)).astype(BF16)
    outs = []
    for hd in range(X_HEADS):
        sl = slice(hd * X_HEAD_DIM, (hd + 1) * X_HEAD_DIM)
        s = _dot_nt(q[:, sl], k_ref[:, sl])
        e = jnp.exp(s - jnp.max(s, axis=-1, keepdims=True))
        p = e / jnp.sum(e, axis=-1, keepdims=True)
        outs.append(_dot(p.astype(BF16), v_ref[:, sl]).astype(BF16))
    o = jnp.concatenate(outs, axis=1)
    y = _dot(o, wo_ref[...])
    o_ref[...] = x + _rms(y, gpost_ref[...])


def _xattn(x, kv, g_pre, w_q, w_o, g_post, batch, seq, tm=512):
    m = x.shape[0]
    tiles = seq // tm
    const = lambda i: (0, 0)
    return pl.pallas_call(
        _xattn_kernel,
        grid=(m // tm,),
        in_specs=[
            pl.BlockSpec((tm, D_MODEL), lambda i: (i, 0)),
            pl.BlockSpec((1, D_MODEL), const),
            pl.BlockSpec((D_MODEL, D_MODEL), const),
            pl.BlockSpec((MEM_LEN, D_MODEL), lambda i: (i // tiles, 0)),
            pl.BlockSpec((MEM_LEN, D_MODEL), lambda i: (i // tiles, 1)),
            pl.BlockSpec((D_MODEL, D_MODEL), const),
            pl.BlockSpec((1, D_MODEL), const),
        ],
        out_specs=pl.BlockSpec((tm, D_MODEL), lambda i: (i, 0)),
        out_shape=jax.ShapeDtypeStruct((m, D_MODEL), F32),
        compiler_params=_params("arbitrary"),
        name="xattn",
    )(x, g_pre, w_q, kv, kv, w_o, g_post)


def _ffn_kernel(tiles_per_seq, x_ref, gpre_ref, wa_ref, wb_ref, cwa_ref, cwb_ref, cba_ref, cbb_ref,
                wd_ref, gpost_ref, o_ref, h_scr, acc_scr, taila_scr, tailb_scr):
    i = pl.program_id(0)
    j = pl.program_id(1)
    tm = x_ref.shape[0]

    @pl.when(j == 0)
    def _():
        h_scr[...] = _rms(x_ref[...], gpre_ref[...]).astype(BF16)

    @pl.when(i % tiles_per_seq == 0)
    def _():
        taila_scr[j] = jnp.zeros(taila_scr.shape[1:], F32)
        tailb_scr[j] = jnp.zeros(tailb_scr.shape[1:], F32)

    h = h_scr[...]
    a_raw = _dot(h, wa_ref[...])
    b_raw = _dot(h, wb_ref[...])
    a = _causal_conv(a_raw, taila_scr[j], cwa_ref[...], cba_ref[...])
    b = _causal_conv(b_raw, tailb_scr[j], cwb_ref[...], cbb_ref[...])
    taila_scr[j] = a_raw[tm - TAIL_ROWS:, :]
    tailb_scr[j] = b_raw[tm - TAIL_ROWS:, :]
    contrib = _dot((_gelu(a) * b).astype(BF16), wd_ref[...])

    @pl.when(j == 0)
    def _():
        acc_scr[...] = contrib

    @pl.when(j > 0)
    def _():
        acc_scr[...] += contrib

    @pl.when(j == pl.num_programs(1) - 1)
    def _():
        o_ref[...] = x_ref[...] + _rms(acc_scr[...], gpost_ref[...])


def _ffn(x, g_pre, w_up, conv_w, conv_b, w_down, g_post, seq, tm=512, tn=1408):
    m = x.shape[0]
    nj = FFN_DIM // tn
    const = lambda i, j: (0, 0)
    kern = functools.partial(_ffn_kernel, seq // tm)
    return pl.pallas_call(
        kern,
        grid=(m // tm, nj),
        in_specs=[
            pl.BlockSpec((tm, D_MODEL), lambda i, j: (i, 0)),
            pl.BlockSpec((1, D_MODEL), const),
            pl.BlockSpec((D_MODEL, tn), lambda i, j: (0, j)),
            pl.BlockSpec((D_MODEL, tn), lambda i, j: (0, j + nj)),
            pl.BlockSpec((FFN_CONV, tn), lambda i, j: (0, j)),
            pl.BlockSpec((FFN_CONV, tn), lambda i, j: (0, j + nj)),
            pl.BlockSpec((1, tn), lambda i, j: (0, j)),
            pl.BlockSpec((1, tn), lambda i, j: (0, j + nj)),
            pl.BlockSpec((tn, D_MODEL), lambda i, j: (j, 0)),
            pl.BlockSpec((1, D_MODEL), const),
        ],
        out_specs=pl.BlockSpec((tm, D_MODEL), lambda i, j: (i, 0)),
        out_shape=jax.ShapeDtypeStruct((m, D_MODEL), F32),
        scratch_shapes=[
            pltpu.VMEM((tm, D_MODEL), BF16),
            pltpu.VMEM((tm, D_MODEL), F32),
            pltpu.VMEM((nj, TAIL_ROWS, tn), F32),
            pltpu.VMEM((nj, TAIL_ROWS, tn), F32),
        ],
        compiler_params=_params("arbitrary", "arbitrary"),
        name="ffn",
    )(x, g_pre, w_up, w_up, conv_w, conv_w, conv_b, conv_b, w_down, g_post)


def _row(v):
    return v.reshape(1, -1)


def _pad_lanes(v):
    return jnp.pad(v, (0, LANES - v.shape[0])).reshape(1, LANES)


def kernel(x, mem, norm_pre_mix, norm_post_mix, norm_pre_mem, norm_mem_kv, norm_post_mem,
           norm_pre_ffn, norm_post_ffn, w_in, gm_v_norm, gm_w_s, gm_b_s, ssd_conv_w, ssd_conv_b,
           ssd_dt_bias, ssd_a_log, ssd_d, ssd_norm, w_branch_a, w_branch_b, w_out,
           xa_w_q, xa_w_kv, xa_w_o, ffn_w_up, ffn_conv_w, ffn_conv_b, ffn_w_down):
    batch, seq, _ = x.shape
    depth = w_in.shape[0]
    xf = x.reshape(batch * seq, D_MODEL)
    memf = mem.reshape(batch * MEM_LEN, D_MODEL)
    e_mat = (lax.broadcasted_iota(jnp.int32, (LANES, SSD_INNER), 0)
             == lax.broadcasted_iota(jnp.int32, (LANES, SSD_INNER), 1) // SSD_HEAD_DIM).astype(BF16)

    for l in range(depth):
        w_main = jnp.concatenate([w_in[l, :, :DT_OFF], w_in[l, :, DT_OFF + SSD_HEADS:]],
                                 axis=1).astype(BF16)
        w_dt = jnp.pad(w_in[l, :, DT_OFF:DT_OFF + SSD_HEADS],
                       ((0, 0), (0, LANES - SSD_HEADS))).astype(BF16)
        proj, dt = _inproj(xf, _row(norm_pre_mix[l]), w_main, w_dt)

        b_exp = jnp.repeat(gm_b_s[l].T, LANES, axis=1)
        mixer_w = (
            _row(gm_v_norm[l]), gm_w_s[l], b_exp,
            ssd_conv_w[l, :, :SSD_INNER], _row(ssd_conv_b[l, :SSD_INNER]),
            ssd_conv_w[l, :, SSD_INNER:], _row(ssd_conv_b[l, SSD_INNER:]),
            _pad_lanes(ssd_dt_bias[l]), _pad_lanes(ssd_a_log[l]),
            _row(jnp.repeat(ssd_d[l], SSD_HEAD_DIM)), _row(ssd_norm[l]), e_mat,
        )
        ya, yb = _mixer(proj, dt, mixer_w, batch, seq)
        xf = _merge(ya, yb, proj, xf, w_branch_a[l].astype(BF16), w_branch_b[l].astype(BF16),
                    w_out[l].astype(BF16), _row(norm_post_mix[l]))

        kv = _kvproj(memf, _row(norm_mem_kv[l]), xa_w_kv[l].astype(BF16))
        xf = _xattn(xf, kv, _row(norm_pre_mem[l]), xa_w_q[l].astype(BF16),
                    xa_w_o[l].astype(BF16), _row(norm_post_mem[l]), batch, seq)

        xf = _ffn(xf, _row(norm_pre_ffn[l]), ffn_w_up[l].astype(BF16), ffn_conv_w[l],
                  _row(ffn_conv_b[l]), ffn_w_down[l].astype(BF16), _row(norm_post_ffn[l]), seq)
    return xf.reshape(batch, seq, D_MODEL)
```

```python
import functools

import jax
import jax.numpy as jnp
from jax import lax
from jax.experimental import pallas as pl
from jax.experimental.pallas import tpu as pltpu

F32 = jnp.float32
BF16 = jnp.bfloat16

D_MODEL = 1024
MEM_LEN = 256
EPS = 1e-6
GM_GROUPS = 8
GM_WIDTH = 1024
CHUNK = 128
SSD_INNER = 2048
SSD_HEAD_DIM = 64
SSD_HEADS = 32
SSD_GROUPS = 4
SSD_HPG = 8
SSD_STATE = 128
SSD_CONV = 4
SSD_BC = 2 * SSD_GROUPS * SSD_STATE
GROUP_COLS = SSD_INNER // SSD_GROUPS
X_HEADS = 4
X_HEAD_DIM = 256
FFN_DIM = 2816
FFN_CONV = 3
LANES = 128
TAIL_ROWS = 8
MAIN_COLS = 9216
DT_OFF = 7168
SEG_OFF = (0, 1024, 2048, 4096, 7168, 9216)
PROJ_SUB = 256

VMEM_LIMIT = 56 * 1024 * 1024


def _params(*sem):
    return pltpu.CompilerParams(dimension_semantics=sem, vmem_limit_bytes=VMEM_LIMIT)


def _dot(a, b):
    return jnp.dot(a, b, preferred_element_type=F32)


def _dot_nt(a, b):
    return lax.dot_general(a, b, (((1,), (1,)), ((), ())), preferred_element_type=F32)


def _rms(x, g):
    ms = jnp.mean(x * x, axis=-1, keepdims=True)
    return x * lax.rsqrt(ms + EPS) * g


def _gelu(x):
    return jax.nn.gelu(x, approximate=True)


def _sigmoid(x):
    return 0.5 * jnp.tanh(0.5 * x) + 0.5


def _silu(x):
    half = 0.5 * x
    return half * jnp.tanh(half) + half


def _softplus(x):
    return jnp.maximum(x, 0.0) + jnp.log1p(jnp.exp(-jnp.abs(x)))


def _causal_conv(cur, tail, w, b):
    k_width = w.shape[0]
    row = lax.broadcasted_iota(jnp.int32, tail.shape, 0)
    y = b + w[k_width - 1:k_width, :] * cur
    for s in range(1, k_width):
        r = pltpu.roll(cur, s, axis=0)
        rt = pltpu.roll(tail, s, axis=0)
        head = jnp.where(row < s, rt, r[0:TAIL_ROWS, :])
        r = jnp.concatenate([head, r[TAIL_ROWS:, :]], axis=0)
        y = y + w[k_width - 1 - s:k_width - s, :] * r
    return y


def _inproj_kernel(tiles_per_seq, x_ref, g_ref, w_ref, wdt_ref, vgain_ref, cw_ref, cb_ref,
                   o_ref, dt_ref, h_scr, v_scr, tail_scr):
    i = pl.program_id(0)
    h = _rms(x_ref[...], g_ref[...]).astype(BF16)
    h_scr[...] = h
    dt_ref[...] = _dot(h, wdt_ref[...])

    @pl.when(i % tiles_per_seq == 0)
    def _():
        tail_scr[...] = jnp.zeros_like(tail_scr)

    def mm(c0):
        return _dot(h_scr[...], w_ref[:, c0:c0 + PROJ_SUB])

    def cols(seg):
        return range(SEG_OFF[seg], SEG_OFF[seg + 1], PROJ_SUB)

    for c0 in cols(0):
        o_ref[:, c0:c0 + PROJ_SUB] = _gelu(mm(c0)).astype(BF16)

    ssq = None
    for c0 in cols(1):
        gv = _gelu(mm(c0))
        v_scr[:, c0 - SEG_OFF[1]:c0 - SEG_OFF[1] + PROJ_SUB] = gv
        part = jnp.sum(gv * gv, axis=-1, keepdims=True)
        ssq = part if ssq is None else ssq + part
    scale = lax.rsqrt(ssq * (1.0 / GM_WIDTH) + EPS)
    for c0 in cols(1):
        lc = c0 - SEG_OFF[1]
        o_ref[:, c0:c0 + PROJ_SUB] = (v_scr[:, lc:lc + PROJ_SUB] * scale
                                      * vgain_ref[:, lc:lc + PROJ_SUB]).astype(BF16)

    for c0 in cols(2):
        o_ref[:, c0:c0 + PROJ_SUB] = _silu(mm(c0)).astype(BF16)

    tm = x_ref.shape[0]
    for c0 in cols(3):
        lc = c0 - SEG_OFF[3]
        raw = mm(c0)
        y = _causal_conv(raw, tail_scr[:, lc:lc + PROJ_SUB], cw_ref[:, lc:lc + PROJ_SUB],
                         cb_ref[:, lc:lc + PROJ_SUB])
        tail_scr[:, lc:lc + PROJ_SUB] = raw[tm - TAIL_ROWS:, :]
        o_ref[:, c0:c0 + PROJ_SUB] = _silu(y).astype(BF16)

    for c0 in cols(4):
        o_ref[:, c0:c0 + PROJ_SUB] = _sigmoid(mm(c0)).astype(BF16)


def _inproj(x, g, w_main, w_dt, v_gain, conv_w, conv_b, seq, tm=512):
    m = x.shape[0]
    n = w_main.shape[1]
    const = lambda i: (0, 0)
    resident = dict(pipeline_mode=pl.Buffered(1))
    return pl.pallas_call(
        functools.partial(_inproj_kernel, seq // tm),
        grid=(m // tm,),
        in_specs=[
            pl.BlockSpec((tm, D_MODEL), lambda i: (i, 0)),
            pl.BlockSpec((1, D_MODEL), const),
            pl.BlockSpec((D_MODEL, n), const, **resident),
            pl.BlockSpec((D_MODEL, LANES), const, **resident),
            pl.BlockSpec((1, GM_WIDTH), const),
            pl.BlockSpec(conv_w.shape, const),
            pl.BlockSpec(conv_b.shape, const),
        ],
        out_specs=[
            pl.BlockSpec((tm, n), lambda i: (i, 0)),
            pl.BlockSpec((tm, LANES), lambda i: (i, 0)),
        ],
        out_shape=[
            jax.ShapeDtypeStruct((m, n), BF16),
            jax.ShapeDtypeStruct((m, LANES), F32),
        ],
        scratch_shapes=[
            pltpu.VMEM((tm, D_MODEL), BF16),
            pltpu.VMEM((tm, GM_WIDTH), F32),
            pltpu.VMEM((TAIL_ROWS, conv_w.shape[1]), F32),
        ],
        compiler_params=_params("arbitrary"),
        name="inproj",
    )(x, g, w_main, w_dt, v_gain, conv_w, conv_b)


def _kvproj_kernel(x_ref, g_ref, w_ref, o_ref):
    h = _rms(x_ref[...], g_ref[...]).astype(BF16)
    o_ref[...] = _dot(h, w_ref[...]).astype(BF16)


def _kvproj(mem, g, w_kv, tm=512):
    m = mem.shape[0]
    n = w_kv.shape[1]
    return pl.pallas_call(
        _kvproj_kernel,
        grid=(m // tm,),
        in_specs=[
            pl.BlockSpec((tm, D_MODEL), lambda i: (i, 0)),
            pl.BlockSpec((1, D_MODEL), lambda i: (0, 0)),
            pl.BlockSpec((D_MODEL, n), lambda i: (0, 0)),
        ],
        out_specs=pl.BlockSpec((tm, n), lambda i: (i, 0)),
        out_shape=jax.ShapeDtypeStruct((m, n), BF16),
        compiler_params=_params("arbitrary"),
        name="kvproj",
    )(mem, g, w_kv)


def _split2(x):
    hi = x.astype(BF16)
    lo = (x - hi.astype(F32)).astype(BF16)
    return hi, lo


def _split3(x):
    hi = x.astype(BF16)
    r = x - hi.astype(F32)
    mid = r.astype(BF16)
    lo = (r - mid.astype(F32)).astype(BF16)
    return hi, mid, lo


def _mixer_kernel(u_ref, v_ref, z_ref, xs_ref, bc_ref, dt_ref,
                  ws_ref, bexp_ref, dtb_ref, alog_ref, dexp_ref, ng_ref, e_ref,
                  ya_ref, yb_ref,
                  state_scr, wsm_scr):
    c = pl.program_id(1)
    L = CHUNK
    rowi = lax.broadcasted_iota(jnp.int32, (L, L), 0)
    coli = lax.broadcasted_iota(jnp.int32, (L, L), 1)
    causal = rowi >= coli

    @pl.when(c == 0)
    def _():
        state_scr[...] = jnp.zeros_like(state_scr)
        for g in range(GM_GROUPS):
            wsm_scr[g] = jnp.where(causal, ws_ref[g], 0.0).astype(BF16)

    for g in range(GM_GROUPS):
        sl = slice(g * LANES, (g + 1) * LANES)
        mixed = _dot(wsm_scr[g], v_ref[:, sl]) + bexp_ref[:, sl]
        ya_ref[:, sl] = (u_ref[:, sl].astype(F32) * mixed).astype(BF16)

    xs_bf = xs_ref[...]
    xs = xs_bf.astype(F32)
    bc_bf = bc_ref[...]

    lane = lax.broadcasted_iota(jnp.int32, (L, LANES), 1)
    head_lane = lane < SSD_HEADS
    dt = jnp.where(head_lane, _softplus(dt_ref[...] + dtb_ref[...]), 0.0)
    a_row = -jnp.exp(alog_ref[...])
    da = dt * a_row
    tril = jnp.where(causal, 1.0, 0.0).astype(BF16)
    d_hi, d_mid, d_lo = _split3(da)
    a_cs = _dot(tril, d_hi) + _dot(tril, d_mid) + _dot(tril, d_lo)
    a_cs_t = a_cs.T
    dt_t = dt.T
    ea = jnp.exp(a_cs)
    w_st = dt * jnp.exp(a_cs[L - 1:L, :] - a_cs)
    e_mat = e_ref[...]
    ea_hi, ea_lo = _split2(ea)
    ea_exp = _dot(ea_hi, e_mat) + _dot(ea_lo, e_mat)
    ws_hi, ws_lo = _split2(w_st)
    wst_exp = _dot(ws_hi, e_mat) + _dot(ws_lo, e_mat)

    lane_pair = lax.broadcasted_iota(jnp.int32, (L, LANES), 1)
    first_half = lane_pair < SSD_HEAD_DIM
    zero_bf = jnp.zeros((L, LANES), BF16)

    for g in range(SSD_GROUPS):
        gsl = slice(g * GROUP_COLS, (g + 1) * GROUP_COLS)
        bm_bf = bc_bf[:, g * SSD_STATE:(g + 1) * SSD_STATE]
        cm_bf = bc_bf[:, SSD_GROUPS * SSD_STATE + g * SSD_STATE:
                      SSD_GROUPS * SSD_STATE + (g + 1) * SSD_STATE]
        cb = _dot_nt(cm_bf, bm_bf)
        xs_g = xs[:, gsl]
        xs_g_bf = xs_bf[:, gsl]
        state_g = state_scr[:, gsl]
        y_g = _dot(cm_bf, state_g.astype(BF16)) * ea_exp[:, gsl]
        parts = []
        for pr in range(SSD_HPG // 2):
            sc = []
            for hh in range(2):
                h = g * SSD_HPG + 2 * pr + hh
                seg = a_cs[:, h:h + 1] - a_cs_t[h:h + 1, :]
                lm = jnp.exp(jnp.where(causal, seg, -jnp.inf))
                sc.append((cb * lm * dt_t[h:h + 1, :]).astype(BF16))
            lhs = jnp.concatenate(sc, axis=1)
            x_pair = xs_g_bf[:, pr * LANES:(pr + 1) * LANES]
            rhs = jnp.concatenate([jnp.where(first_half, x_pair, zero_bf),
                                   jnp.where(first_half, zero_bf, x_pair)], axis=0)
            parts.append(_dot(lhs, rhs))
        y_g = y_g + jnp.concatenate(parts, axis=1) + xs_g * dexp_ref[:, gsl]
        xw = (xs_g * wst_exp[:, gsl]).astype(BF16)
        new_state = state_g * ea_exp[L - 1:L, gsl] + _dot(bm_bf.astype(F32).T.astype(BF16), xw)
        state_scr[:, gsl] = new_state
        y_g = y_g * z_ref[:, gsl].astype(F32)
        yb_ref[:, gsl] = _rms(y_g, ng_ref[:, gsl]).astype(BF16)


def _mixer(proj, dt, layer_w, batch, seq):
    nc = seq // CHUNK
    row = lambda b, c: b * nc + c
    const2 = lambda b, c: (0, 0)
    in_specs = [
        pl.BlockSpec((CHUNK, GM_WIDTH), lambda b, c: (row(b, c), 0)),
        pl.BlockSpec((CHUNK, GM_WIDTH), lambda b, c: (row(b, c), 1)),
        pl.BlockSpec((CHUNK, SSD_INNER), lambda b, c: (row(b, c), 1)),
        pl.BlockSpec((CHUNK, SSD_INNER), lambda b, c: (row(b, c), 2)),
        pl.BlockSpec((CHUNK, SSD_BC), lambda b, c: (row(b, c), 6)),
        pl.BlockSpec((CHUNK, LANES), lambda b, c: (row(b, c), 0)),
        pl.BlockSpec((GM_GROUPS, CHUNK, CHUNK), lambda b, c: (0, 0, 0)),
        pl.BlockSpec((CHUNK, GM_WIDTH), const2),
        pl.BlockSpec((1, LANES), const2),
        pl.BlockSpec((1, LANES), const2),
        pl.BlockSpec((1, SSD_INNER), const2),
        pl.BlockSpec((1, SSD_INNER), const2),
        pl.BlockSpec((LANES, SSD_INNER), const2),
    ]
    m = batch * seq
    return pl.pallas_call(
        _mixer_kernel,
        grid=(batch, nc),
        in_specs=in_specs,
        out_specs=[
            pl.BlockSpec((CHUNK, GM_WIDTH), lambda b, c: (row(b, c), 0)),
            pl.BlockSpec((CHUNK, SSD_INNER), lambda b, c: (row(b, c), 0)),
        ],
        out_shape=[
            jax.ShapeDtypeStruct((m, GM_WIDTH), BF16),
            jax.ShapeDtypeStruct((m, SSD_INNER), BF16),
        ],
        scratch_shapes=[
            pltpu.VMEM((SSD_STATE, SSD_INNER), F32),
            pltpu.VMEM((GM_GROUPS, CHUNK, CHUNK), BF16),
        ],
        compiler_params=_params("arbitrary", "arbitrary"),
        name="mixer",
    )(proj, proj, proj, proj, proj, dt, *layer_w)


def _merge_kernel(ya_ref, yb_ref, ga_ref, gb_ref, x_ref, wa_ref, wb_ref, wo_ref, g_ref, o_ref):
    ya = _dot(ya_ref[...], wa_ref[...])
    yb = _dot(yb_ref[...], wb_ref[...])
    mixed = ga_ref[...].astype(F32) * ya + gb_ref[...].astype(F32) * yb
    y = _dot(mixed.astype(BF16), wo_ref[...])
    o_ref[...] = x_ref[...] + _rms(y, g_ref[...])


def _merge(ya, yb, proj, x, w_a, w_b, w_o, g, tm=512):
    m = x.shape[0]
    const = lambda i: (0, 0)
    return pl.pallas_call(
        _merge_kernel,
        grid=(m // tm,),
        in_specs=[
            pl.BlockSpec((tm, GM_WIDTH), lambda i: (i, 0)),
            pl.BlockSpec((tm, SSD_INNER), lambda i: (i, 0)),
            pl.BlockSpec((tm, D_MODEL), lambda i: (i, 7)),
            pl.BlockSpec((tm, D_MODEL), lambda i: (i, 8)),
            pl.BlockSpec((tm, D_MODEL), lambda i: (i, 0)),
            pl.BlockSpec((GM_WIDTH, D_MODEL), const),
            pl.BlockSpec((SSD_INNER, D_MODEL), const),
            pl.BlockSpec((D_MODEL, D_MODEL), const),
            pl.BlockSpec((1, D_MODEL), const),
        ],
        out_specs=pl.BlockSpec((tm, D_MODEL), lambda i: (i, 0)),
        out_shape=jax.ShapeDtypeStruct((m, D_MODEL), F32),
        compiler_params=_params("arbitrary"),
        name="merge",
    )(ya, yb, proj, proj, x, w_a, w_b, w_o, g)


def _xattn_kernel(x_ref, gpre_ref, wq_ref, k_ref, v_ref, wo_ref, gpost_ref, o_ref):
    x = x_ref[...]
    h = _rms(x, gpre_ref[...]).astype(BF16)
    q = (_dot(h, wq_ref[...]) * (X_HEAD_DIM ** -0.5)).astype(BF16)
    outs = []
    for hd in range(X_HEADS):
        sl = slice(hd * X_HEAD_DIM, (hd + 1) * X_HEAD_DIM)
        s = _dot_nt(q[:, sl], k_ref[:, sl])
        e = jnp.exp(s - jnp.max(s, axis=-1, keepdims=True))
        p = e / jnp.sum(e, axis=-1, keepdims=True)
        outs.append(_dot(p.astype(BF16), v_ref[:, sl]).astype(BF16))
    o = jnp.concatenate(outs, axis=1)
    y = _dot(o, wo_ref[...])
    o_ref[...] = x + _rms(y, gpost_ref[...])


def _xattn(x, kv, g_pre, w_q, w_o, g_post, batch, seq, tm=512):
    m = x.shape[0]
    tiles = seq // tm
    const = lambda i: (0, 0)
    return pl.pallas_call(
        _xattn_kernel,
        grid=(m // tm,),
        in_specs=[
            pl.BlockSpec((tm, D_MODEL), lambda i: (i, 0)),
            pl.BlockSpec((1, D_MODEL), const),
            pl.BlockSpec((D_MODEL, D_MODEL), const),
            pl.BlockSpec((MEM_LEN, D_MODEL), lambda i: (i // tiles, 0)),
            pl.BlockSpec((MEM_LEN, D_MODEL), lambda i: (i // tiles, 1)),
            pl.BlockSpec((D_MODEL, D_MODEL), const),
            pl.BlockSpec((1, D_MODEL), const),
        ],
        out_specs=pl.BlockSpec((tm, D_MODEL), lambda i: (i, 0)),
        out_shape=jax.ShapeDtypeStruct((m, D_MODEL), F32),
        compiler_params=_params("arbitrary"),
        name="xattn",
    )(x, g_pre, w_q, kv, kv, w_o, g_post)


def _ffn_kernel(tiles_per_seq, x_ref, gpre_ref, wa_ref, wb_ref, cwa_ref, cwb_ref, cba_ref, cbb_ref,
                wd_ref, gpost_ref, o_ref, h_scr, acc_scr, taila_scr, tailb_scr):
    i = pl.program_id(0)
    j = pl.program_id(1)
    tm = x_ref.shape[0]

    @pl.when(j == 0)
    def _():
        h_scr[...] = _rms(x_ref[...], gpre_ref[...]).astype(BF16)

    @pl.when(i % tiles_per_seq == 0)
    def _():
        taila_scr[j] = jnp.zeros(taila_scr.shape[1:], F32)
        tailb_scr[j] = jnp.zeros(tailb_scr.shape[1:], F32)

    h = h_scr[...]
    a_raw = _dot(h, wa_ref[...])
    b_raw = _dot(h, wb_ref[...])
    a = _causal_conv(a_raw, taila_scr[j], cwa_ref[...], cba_ref[...])
    b = _causal_conv(b_raw, tailb_scr[j], cwb_ref[...], cbb_ref[...])
    taila_scr[j] = a_raw[tm - TAIL_ROWS:, :]
    tailb_scr[j] = b_raw[tm - TAIL_ROWS:, :]
    contrib = _dot((_gelu(a) * b).astype(BF16), wd_ref[...])

    @pl.when(j == 0)
    def _():
        acc_scr[...] = contrib

    @pl.when(j > 0)
    def _():
        acc_scr[...] += contrib

    @pl.when(j == pl.num_programs(1) - 1)
    def _():
        o_ref[...] = x_ref[...] + _rms(acc_scr[...], gpost_ref[...])


def _ffn(x, g_pre, w_up, conv_w, conv_b, w_down, g_post, seq, tm=512, tn=1408):
    m = x.shape[0]
    nj = FFN_DIM // tn
    const = lambda i, j: (0, 0)
    kern = functools.partial(_ffn_kernel, seq // tm)
    return pl.pallas_call(
        kern,
        grid=(m // tm, nj),
        in_specs=[
            pl.BlockSpec((tm, D_MODEL), lambda i, j: (i, 0)),
            pl.BlockSpec((1, D_MODEL), const),
            pl.BlockSpec((D_MODEL, tn), lambda i, j: (0, j)),
            pl.BlockSpec((D_MODEL, tn), lambda i, j: (0, j + nj)),
            pl.BlockSpec((FFN_CONV, tn), lambda i, j: (0, j)),
            pl.BlockSpec((FFN_CONV, tn), lambda i, j: (0, j + nj)),
            pl.BlockSpec((1, tn), lambda i, j: (0, j)),
            pl.BlockSpec((1, tn), lambda i, j: (0, j + nj)),
            pl.BlockSpec((tn, D_MODEL), lambda i, j: (j, 0)),
            pl.BlockSpec((1, D_MODEL), const),
        ],
        out_specs=pl.BlockSpec((tm, D_MODEL), lambda i, j: (i, 0)),
        out_shape=jax.ShapeDtypeStruct((m, D_MODEL), F32),
        scratch_shapes=[
            pltpu.VMEM((tm, D_MODEL), BF16),
            pltpu.VMEM((tm, D_MODEL), F32),
            pltpu.VMEM((nj, TAIL_ROWS, tn), F32),
            pltpu.VMEM((nj, TAIL_ROWS, tn), F32),
        ],
        compiler_params=_params("arbitrary", "arbitrary"),
        name="ffn",
    )(x, g_pre, w_up, w_up, conv_w, conv_w, conv_b, conv_b, w_down, g_post)


def _row(v):
    return v.reshape(1, -1)


def _pad_lanes(v):
    return jnp.pad(v, (0, LANES - v.shape[0])).reshape(1, LANES)


def kernel(x, mem, norm_pre_mix, norm_post_mix, norm_pre_mem, norm_mem_kv, norm_post_mem,
           norm_pre_ffn, norm_post_ffn, w_in, gm_v_norm, gm_w_s, gm_b_s, ssd_conv_w, ssd_conv_b,
           ssd_dt_bias, ssd_a_log, ssd_d, ssd_norm, w_branch_a, w_branch_b, w_out,
           xa_w_q, xa_w_kv, xa_w_o, ffn_w_up, ffn_conv_w, ffn_conv_b, ffn_w_down):
    batch, seq, _ = x.shape
    depth = w_in.shape[0]
    xf = x.reshape(batch * seq, D_MODEL)
    memf = mem.reshape(batch * MEM_LEN, D_MODEL)
    e_mat = (lax.broadcasted_iota(jnp.int32, (LANES, SSD_INNER), 0)
             == lax.broadcasted_iota(jnp.int32, (LANES, SSD_INNER), 1) // SSD_HEAD_DIM).astype(BF16)

    for l in range(depth):
        w_main = jnp.concatenate([w_in[l, :, :DT_OFF].astype(BF16),
                                  w_in[l, :, DT_OFF + SSD_HEADS:].astype(BF16)], axis=1)
        w_dt = jnp.pad(w_in[l, :, DT_OFF:DT_OFF + SSD_HEADS],
                       ((0, 0), (0, LANES - SSD_HEADS))).astype(BF16)
        proj, dt = _inproj(xf, _row(norm_pre_mix[l]), w_main, w_dt, _row(gm_v_norm[l]),
                           ssd_conv_w[l], _row(ssd_conv_b[l]), seq)

        b_exp = jnp.repeat(gm_b_s[l].T, LANES, axis=1)
        mixer_w = (
            gm_w_s[l], b_exp,
            _pad_lanes(ssd_dt_bias[l]), _pad_lanes(ssd_a_log[l]),
            _row(jnp.repeat(ssd_d[l], SSD_HEAD_DIM)), _row(ssd_norm[l]), e_mat,
        )
        ya, yb = _mixer(proj, dt, mixer_w, batch, seq)
        xf = _merge(ya, yb, proj, xf, w_branch_a[l].astype(BF16), w_branch_b[l].astype(BF16),
                    w_out[l].astype(BF16), _row(norm_post_mix[l]))

        kv = _kvproj(memf, _row(norm_mem_kv[l]), xa_w_kv[l].astype(BF16))
        xf = _xattn(xf, kv, _row(norm_pre_mem[l]), xa_w_q[l].astype(BF16),
                    xa_w_o[l].astype(BF16), _row(norm_post_mem[l]), batch, seq)

        xf = _ffn(xf, _row(norm_pre_ffn[l]), ffn_w_up[l].astype(BF16), ffn_conv_w[l],
                  _row(ffn_conv_b[l]), ffn_w_down[l].astype(BF16), _row(norm_post_ffn[l]), seq)
    return xf.reshape(batch, seq, D_MODEL)
```

```python
import functools

import jax
import jax.numpy as jnp
from jax import lax
from jax.experimental import pallas as pl
from jax.experimental.pallas import tpu as pltpu

F32 = jnp.float32
BF16 = jnp.bfloat16

D_MODEL = 1024
MEM_LEN = 256
EPS = 1e-6
GM_GROUPS = 8
GM_WIDTH = 1024
CHUNK = 128
SSD_INNER = 2048
SSD_HEAD_DIM = 64
SSD_HEADS = 32
SSD_GROUPS = 4
SSD_HPG = 8
SSD_STATE = 128
SSD_CONV = 4
SSD_XBC = 3072
SSD_BC = 2 * SSD_GROUPS * SSD_STATE
GROUP_COLS = SSD_INNER // SSD_GROUPS
X_HEADS = 4
X_HEAD_DIM = 256
FFN_DIM = 2816
FFN_CONV = 3
LANES = 128
MXU_COLS = 256
TAIL_ROWS = 8
MAIN_COLS = 9216
DT_OFF = 7168
SEG_OFF = (0, 1024, 2048, 4096, 7168, 9216)
STAGE_DEPTH = 2
FFN_DOWN_GROUP = 4

VMEM_LIMIT = 56 * 1024 * 1024


def _params(*sem):
    return pltpu.CompilerParams(dimension_semantics=sem, vmem_limit_bytes=VMEM_LIMIT)


def _layer_spec(layer, *shape, resident=False):
    kw = dict(pipeline_mode=pl.Buffered(1)) if resident else {}
    return pl.BlockSpec((None,) + shape, lambda *_: (layer,) + (0,) * len(shape), **kw)


def _dot(a, b):
    return jnp.dot(a, b, preferred_element_type=F32)


def _dot_nt(a, b):
    return lax.dot_general(a, b, (((1,), (1,)), ((), ())), preferred_element_type=F32)


def _rms(x, g):
    ms = jnp.mean(x * x, axis=-1, keepdims=True)
    return x * lax.rsqrt(ms + EPS) * g


GELU_C1 = 0.7978845608028654
GELU_C2 = GELU_C1 * 0.044715


def _gelu(x):
    half = 0.5 * x
    return half * jnp.tanh(x * (GELU_C1 + GELU_C2 * (x * x))) + half


def _sigmoid(x):
    return 0.5 * jnp.tanh(0.5 * x) + 0.5


def _silu(x):
    half = 0.5 * x
    return half * jnp.tanh(half) + half


def _softplus(x):
    return jnp.maximum(x, 0.0) + jnp.log1p(jnp.exp(-jnp.abs(x)))


def _causal_conv(cur, tail, w, b):
    k_width = w.shape[0]
    row = lax.broadcasted_iota(jnp.int32, tail.shape, 0)
    y = b + w[k_width - 1:k_width, :] * cur
    for s in range(1, k_width):
        r = pltpu.roll(cur, s, axis=0)
        rt = pltpu.roll(tail, s, axis=0)
        head = jnp.where(row < s, rt, r[0:TAIL_ROWS, :])
        r = jnp.concatenate([head, r[TAIL_ROWS:, :]], axis=0)
        y = y + w[k_width - 1 - s:k_width - s, :] * r
    return y


def _interleave(a, b):
    out = []
    for k in range(max(len(a), len(b))):
        out.extend(a[k:k + 1])
        out.extend(b[k:k + 1])
    return out


def _staged(n, produce, consume, bufs, slot0):
    assert len(bufs) == STAGE_DEPTH + 1
    for k in range(n + STAGE_DEPTH):
        if k < n:
            bufs[k % len(bufs)][slot0] = produce(k)
        if k >= STAGE_DEPTH:
            consume(k - STAGE_DEPTH, bufs[(k - STAGE_DEPTH) % len(bufs)][slot0])


def _inproj_kernel(tiles_per_seq, x_ref, g_ref, w_ref, wdt_ref, vgain_ref, cw_ref, cb_ref,
                   o_ref, dt_ref, h_scr, v_scr, tail_scr, *stage):
    i = pl.program_id(0)
    tm = x_ref.shape[0]
    sub = MXU_COLS
    h = _rms(x_ref[...], g_ref[...]).astype(BF16)
    h_scr[...] = h
    dt_ref[...] = _dot(h, wdt_ref[...])

    @pl.when(i % tiles_per_seq == 0)
    def _():
        tail_scr[...] = jnp.zeros_like(tail_scr)

    ssq = []

    def ep_u(raw, c0):
        o_ref[:, c0:c0 + sub] = _gelu(raw).astype(BF16)

    def ep_v(raw, c0):
        lc = c0 - SEG_OFF[1]
        gv = _gelu(raw)
        v_scr[:, lc:lc + sub] = gv
        ssq.append(jnp.sum(gv * gv, axis=-1, keepdims=True))

    def ep_z(raw, c0):
        o_ref[:, c0:c0 + sub] = _silu(raw).astype(BF16)

    def ep_xbc(raw, c0):
        lc = c0 - SEG_OFF[3]
        y = _causal_conv(raw, tail_scr[:, lc:lc + sub], cw_ref[:, lc:lc + sub], cb_ref[:, lc:lc + sub])
        tail_scr[:, lc:lc + sub] = raw[tm - TAIL_ROWS:, :]
        o_ref[:, c0:c0 + sub] = _silu(y).astype(BF16)

    def ep_gate(raw, c0):
        o_ref[:, c0:c0 + sub] = _sigmoid(raw).astype(BF16)

    def seg_tasks(*pairs):
        return [(fn, c0) for seg, fn in pairs for c0 in range(SEG_OFF[seg], SEG_OFF[seg + 1], sub)]

    tasks = _interleave(seg_tasks((3, ep_xbc), (1, ep_v), (0, ep_u)),
                        seg_tasks((2, ep_z), (4, ep_gate)))
    _staged(len(tasks),
            lambda k: _dot(h_scr[...], w_ref[:, tasks[k][1]:tasks[k][1] + sub]),
            lambda k, raw: tasks[k][0](raw, tasks[k][1]),
            stage, jnp.minimum(i, 0))

    scale = lax.rsqrt(sum(ssq) * (1.0 / GM_WIDTH) + EPS)
    for c0 in range(SEG_OFF[1], SEG_OFF[2], sub):
        lc = c0 - SEG_OFF[1]
        o_ref[:, c0:c0 + sub] = (v_scr[:, lc:lc + sub] * scale * vgain_ref[:, lc:lc + sub]).astype(BF16)


def _inproj(layer, x, g, w_main, w_dt, v_gain, conv_w, conv_b, seq, tm=512):
    m = x.shape[0]
    return pl.pallas_call(
        functools.partial(_inproj_kernel, seq // tm),
        grid=(m // tm,),
        in_specs=[
            pl.BlockSpec((tm, D_MODEL), lambda i: (i, 0)),
            _layer_spec(layer, 1, D_MODEL),
            _layer_spec(layer, D_MODEL, MAIN_COLS, resident=True),
            _layer_spec(layer, D_MODEL, LANES, resident=True),
            _layer_spec(layer, 1, GM_WIDTH),
            _layer_spec(layer, SSD_CONV, SSD_XBC),
            _layer_spec(layer, 1, SSD_XBC),
        ],
        out_specs=[
            pl.BlockSpec((tm, MAIN_COLS), lambda i: (i, 0)),
            pl.BlockSpec((tm, LANES), lambda i: (i, 0)),
        ],
        out_shape=[
            jax.ShapeDtypeStruct((m, MAIN_COLS), BF16),
            jax.ShapeDtypeStruct((m, LANES), F32),
        ],
        scratch_shapes=[
            pltpu.VMEM((tm, D_MODEL), BF16),
            pltpu.VMEM((tm, GM_WIDTH), F32),
            pltpu.VMEM((TAIL_ROWS, SSD_XBC), F32),
        ] + [pltpu.VMEM((1, tm, MXU_COLS), F32)] * (STAGE_DEPTH + 1),
        compiler_params=_params("arbitrary"),
        name="inproj",
    )(x, g, w_main, w_dt, v_gain, conv_w, conv_b)


def _kvproj_kernel(x_ref, g_ref, w_ref, o_ref):
    h = _rms(x_ref[...], g_ref[...]).astype(BF16)
    o_ref[...] = _dot(h, w_ref[...]).astype(BF16)


def _kvproj(layer, mem, g, w_kv, tm=512):
    m = mem.shape[0]
    n = 2 * D_MODEL
    return pl.pallas_call(
        _kvproj_kernel,
        grid=(m // tm,),
        in_specs=[
            pl.BlockSpec((tm, D_MODEL), lambda i: (i, 0)),
            _layer_spec(layer, 1, D_MODEL),
            _layer_spec(layer, D_MODEL, n),
        ],
        out_specs=pl.BlockSpec((tm, n), lambda i: (i, 0)),
        out_shape=jax.ShapeDtypeStruct((m, n), BF16),
        compiler_params=_params("arbitrary"),
        name="kvproj",
    )(mem, g, w_kv)


def _split2(x):
    hi = x.astype(BF16)
    lo = (x - hi.astype(F32)).astype(BF16)
    return hi, lo


def _split3(x):
    hi = x.astype(BF16)
    r = x - hi.astype(F32)
    mid = r.astype(BF16)
    lo = (r - mid.astype(F32)).astype(BF16)
    return hi, mid, lo


def _mixer_kernel(u_ref, v_ref, z_ref, xs_ref, bc_ref, dt_ref,
                  ws_ref, bexp_ref, dtb_ref, alog_ref, dexp_ref, ng_ref, e_ref,
                  ya_ref, yb_ref,
                  state_scr, wsm_scr):
    c = pl.program_id(1)
    L = CHUNK
    rowi = lax.broadcasted_iota(jnp.int32, (L, L), 0)
    coli = lax.broadcasted_iota(jnp.int32, (L, L), 1)
    causal = rowi >= coli

    @pl.when(c == 0)
    def _():
        state_scr[...] = jnp.zeros_like(state_scr)
        for g in range(GM_GROUPS):
            wsm_scr[g] = jnp.where(causal, ws_ref[g], 0.0).astype(BF16)

    for g in range(GM_GROUPS):
        sl = slice(g * LANES, (g + 1) * LANES)
        mixed = _dot(wsm_scr[g], v_ref[:, sl]) + bexp_ref[:, sl]
        ya_ref[:, sl] = (u_ref[:, sl].astype(F32) * mixed).astype(BF16)

    lane = lax.broadcasted_iota(jnp.int32, (L, LANES), 1)
    head_lane = lane < SSD_HEADS
    dt = jnp.where(head_lane, _softplus(dt_ref[...] + dtb_ref[...]), 0.0)
    a_row = -jnp.exp(alog_ref[...])
    da = dt * a_row
    tril = jnp.where(causal, 1.0, 0.0).astype(BF16)
    d_hi, d_mid, d_lo = _split3(da)
    a_cs = _dot(tril, d_hi) + _dot(tril, d_mid) + _dot(tril, d_lo)
    a_cs_t = a_cs.T
    dt_t = dt.T
    ea_hi, ea_lo = _split2(jnp.exp(a_cs))
    ws_hi, ws_lo = _split2(dt * jnp.exp(a_cs[L - 1:L, :] - a_cs))

    first_half = lane < SSD_HEAD_DIM
    zero_bf = jnp.zeros((L, LANES), BF16)

    for g in range(SSD_GROUPS):
        gsl = slice(g * GROUP_COLS, (g + 1) * GROUP_COLS)
        bm_bf = bc_ref[:, g * SSD_STATE:(g + 1) * SSD_STATE]
        cm_bf = bc_ref[:, SSD_GROUPS * SSD_STATE + g * SSD_STATE:
                       SSD_GROUPS * SSD_STATE + (g + 1) * SSD_STATE]
        cb = _dot_nt(cm_bf, bm_bf)
        xs_g_bf = xs_ref[:, gsl]
        xs_g = xs_g_bf.astype(F32)
        e_g = e_ref[:, gsl]
        ea_exp = _dot(ea_hi, e_g) + _dot(ea_lo, e_g)
        wst_exp = _dot(ws_hi, e_g) + _dot(ws_lo, e_g)
        state_g = state_scr[:, gsl]
        y_g = _dot(cm_bf, state_g.astype(BF16)) * ea_exp
        parts = []
        for pr in range(SSD_HPG // 2):
            sc = []
            for hh in range(2):
                h = g * SSD_HPG + 2 * pr + hh
                seg = a_cs[:, h:h + 1] - a_cs_t[h:h + 1, :]
                lm = jnp.exp(jnp.where(causal, seg, -jnp.inf))
                sc.append((cb * lm * dt_t[h:h + 1, :]).astype(BF16))
            lhs = jnp.concatenate(sc, axis=1)
            x_pair = xs_g_bf[:, pr * LANES:(pr + 1) * LANES]
            rhs = jnp.concatenate([jnp.where(first_half, x_pair, zero_bf),
                                   jnp.where(first_half, zero_bf, x_pair)], axis=0)
            parts.append(_dot(lhs, rhs))
        y_g = y_g + jnp.concatenate(parts, axis=1) + xs_g * dexp_ref[:, gsl]
        xw = (xs_g * wst_exp).astype(BF16)
        new_state = state_g * ea_exp[L - 1:L, :] + _dot(bm_bf.astype(F32).T.astype(BF16), xw)
        state_scr[:, gsl] = new_state
        y_g = y_g * z_ref[:, gsl].astype(F32)
        yb_ref[:, gsl] = _rms(y_g, ng_ref[:, gsl]).astype(BF16)


def _mixer(layer, proj, dt, w_s, b_exp, dt_bias, a_log, d_exp, norm_g, e_mat, batch, seq):
    nc = seq // CHUNK
    row = lambda b, c: b * nc + c
    in_specs = [
        pl.BlockSpec((CHUNK, GM_WIDTH), lambda b, c: (row(b, c), 0)),
        pl.BlockSpec((CHUNK, GM_WIDTH), lambda b, c: (row(b, c), 1)),
        pl.BlockSpec((CHUNK, SSD_INNER), lambda b, c: (row(b, c), 1)),
        pl.BlockSpec((CHUNK, SSD_INNER), lambda b, c: (row(b, c), 2)),
        pl.BlockSpec((CHUNK, SSD_BC), lambda b, c: (row(b, c), 6)),
        pl.BlockSpec((CHUNK, LANES), lambda b, c: (row(b, c), 0)),
        _layer_spec(layer, GM_GROUPS, CHUNK, CHUNK),
        _layer_spec(layer, CHUNK, GM_WIDTH),
        _layer_spec(layer, 1, LANES),
        _layer_spec(layer, 1, LANES),
        _layer_spec(layer, 1, SSD_INNER),
        _layer_spec(layer, 1, SSD_INNER),
        pl.BlockSpec((LANES, SSD_INNER), lambda b, c: (0, 0)),
    ]
    m = batch * seq
    return pl.pallas_call(
        _mixer_kernel,
        grid=(batch, nc),
        in_specs=in_specs,
        out_specs=[
            pl.BlockSpec((CHUNK, GM_WIDTH), lambda b, c: (row(b, c), 0)),
            pl.BlockSpec((CHUNK, SSD_INNER), lambda b, c: (row(b, c), 0)),
        ],
        out_shape=[
            jax.ShapeDtypeStruct((m, GM_WIDTH), BF16),
            jax.ShapeDtypeStruct((m, SSD_INNER), BF16),
        ],
        scratch_shapes=[
            pltpu.VMEM((SSD_STATE, SSD_INNER), F32),
            pltpu.VMEM((GM_GROUPS, CHUNK, CHUNK), BF16),
        ],
        compiler_params=_params("arbitrary", "arbitrary"),
        name="mixer",
    )(proj, proj, proj, proj, proj, dt, w_s, b_exp, dt_bias, a_log, d_exp, norm_g, e_mat)


def _merge_kernel(ya_ref, yb_ref, ga_ref, gb_ref, x_ref, wa_ref, wb_ref, wo_ref, g_ref, o_ref):
    ya = _dot(ya_ref[...], wa_ref[...])
    yb = _dot(yb_ref[...], wb_ref[...])
    mixed = ga_ref[...].astype(F32) * ya + gb_ref[...].astype(F32) * yb
    y = _dot(mixed.astype(BF16), wo_ref[...])
    o_ref[...] = x_ref[...] + _rms(y, g_ref[...])


def _merge(layer, ya, yb, proj, x, w_a, w_b, w_o, g, tm=512):
    m = x.shape[0]
    return pl.pallas_call(
        _merge_kernel,
        grid=(m // tm,),
        in_specs=[
            pl.BlockSpec((tm, GM_WIDTH), lambda i: (i, 0)),
            pl.BlockSpec((tm, SSD_INNER), lambda i: (i, 0)),
            pl.BlockSpec((tm, D_MODEL), lambda i: (i, 7)),
            pl.BlockSpec((tm, D_MODEL), lambda i: (i, 8)),
            pl.BlockSpec((tm, D_MODEL), lambda i: (i, 0)),
            _layer_spec(layer, GM_WIDTH, D_MODEL),
            _layer_spec(layer, SSD_INNER, D_MODEL),
            _layer_spec(layer, D_MODEL, D_MODEL),
            _layer_spec(layer, 1, D_MODEL),
        ],
        out_specs=pl.BlockSpec((tm, D_MODEL), lambda i: (i, 0)),
        out_shape=jax.ShapeDtypeStruct((m, D_MODEL), F32),
        compiler_params=_params("arbitrary"),
        name="merge",
    )(ya, yb, proj, proj, x, w_a, w_b, w_o, g)


def _xattn_kernel(x_ref, gpre_ref, wq_ref, k_ref, v_ref, wo_ref, gpost_ref, o_ref):
    x = x_ref[...]
    h = _rms(x, gpre_ref[...]).astype(BF16)
    q = (_dot(h, wq_ref[...]) * (X_HEAD_DIM ** -0.5)).astype(BF16)
    outs = []
    for hd in range(X_HEADS):
        sl = slice(hd * X_HEAD_DIM, (hd + 1) * X_HEAD_DIM)
        s = _dot_nt(q[:, sl], k_ref[:, sl])
        e = jnp.exp(s - jnp.max(s, axis=-1, keepdims=True))
        p = e / jnp.sum(e, axis=-1, keepdims=True)
        outs.append(_dot(p.astype(BF16), v_ref[:, sl]).astype(BF16))
    o = jnp.concatenate(outs, axis=1)
    y = _dot(o, wo_ref[...])
    o_ref[...] = x + _rms(y, gpost_ref[...])


def _xattn(layer, x, kv, g_pre, w_q, w_o, g_post, seq, tm=512):
    m = x.shape[0]
    tiles = seq // tm
    return pl.pallas_call(
        _xattn_kernel,
        grid=(m // tm,),
        in_specs=[
            pl.BlockSpec((tm, D_MODEL), lambda i: (i, 0)),
            _layer_spec(layer, 1, D_MODEL),
            _layer_spec(layer, D_MODEL, D_MODEL),
            pl.BlockSpec((MEM_LEN, D_MODEL), lambda i: (i // tiles, 0)),
            pl.BlockSpec((MEM_LEN, D_MODEL), lambda i: (i // tiles, 1)),
            _layer_spec(layer, D_MODEL, D_MODEL),
            _layer_spec(layer, 1, D_MODEL),
        ],
        out_specs=pl.BlockSpec((tm, D_MODEL), lambda i: (i, 0)),
        out_shape=jax.ShapeDtypeStruct((m, D_MODEL), F32),
        compiler_params=_params("arbitrary"),
        name="xattn",
    )(x, g_pre, w_q, kv, kv, w_o, g_post)


def _ffn_kernel(tiles_per_seq, x_ref, gpre_ref, wup_ref, cw_ref, cb_ref, wd_ref, gpost_ref, o_ref,
                h_scr, g_scr, tail_scr, *stage):
    i = pl.program_id(0)
    tm = x_ref.shape[0]
    sub = MXU_COLS
    h_scr[...] = _rms(x_ref[...], gpre_ref[...]).astype(BF16)

    @pl.when(i % tiles_per_seq == 0)
    def _():
        tail_scr[...] = jnp.zeros_like(tail_scr)

    def produce(k):
        c0 = (k // 2) * sub + (k % 2) * FFN_DIM
        return _dot(h_scr[...], wup_ref[:, c0:c0 + sub])

    pending = {}
    down = []
    n_sub = FFN_DIM // sub

    def consume(k, raw):
        c0 = (k // 2) * sub + (k % 2) * FFN_DIM
        y = _causal_conv(raw, tail_scr[:, c0:c0 + sub], cw_ref[:, c0:c0 + sub], cb_ref[:, c0:c0 + sub])
        tail_scr[:, c0:c0 + sub] = raw[tm - TAIL_ROWS:, :]
        if k % 2 == 0:
            pending[k // 2] = _gelu(y)
            return
        j = k // 2
        g_scr[:, j * sub:(j + 1) * sub] = (pending.pop(j) * y).astype(BF16)
        if (j + 1) % FFN_DOWN_GROUP == 0 or j == n_sub - 1:
            lo = (j // FFN_DOWN_GROUP) * FFN_DOWN_GROUP * sub
            down.append(_dot(g_scr[:, lo:(j + 1) * sub], wd_ref[lo:(j + 1) * sub, :]))

    _staged(2 * n_sub, produce, consume, stage, jnp.minimum(i, 0))

    o_ref[...] = x_ref[...] + _rms(sum(down), gpost_ref[...])


def _ffn(layer, x, g_pre, w_up, conv_w, conv_b, w_down, g_post, seq, tm=512):
    m = x.shape[0]
    return pl.pallas_call(
        functools.partial(_ffn_kernel, seq // tm),
        grid=(m // tm,),
        in_specs=[
            pl.BlockSpec((tm, D_MODEL), lambda i: (i, 0)),
            _layer_spec(layer, 1, D_MODEL),
            _layer_spec(layer, D_MODEL, 2 * FFN_DIM, resident=True),
            _layer_spec(layer, FFN_CONV, 2 * FFN_DIM),
            _layer_spec(layer, 1, 2 * FFN_DIM),
            _layer_spec(layer, FFN_DIM, D_MODEL, resident=True),
            _layer_spec(layer, 1, D_MODEL),
        ],
        out_specs=pl.BlockSpec((tm, D_MODEL), lambda i: (i, 0)),
        out_shape=jax.ShapeDtypeStruct((m, D_MODEL), F32),
        scratch_shapes=[
            pltpu.VMEM((tm, D_MODEL), BF16),
            pltpu.VMEM((tm, FFN_DIM), BF16),
            pltpu.VMEM((TAIL_ROWS, 2 * FFN_DIM), F32),
        ] + [pltpu.VMEM((1, tm, MXU_COLS), F32)] * (STAGE_DEPTH + 1),
        compiler_params=_params("arbitrary"),
        name="ffn",
    )(x, g_pre, w_up, conv_w, conv_b, w_down, g_post)


def _rows(v):
    return v.reshape(v.shape[0], 1, v.shape[1])


def kernel(x, mem, norm_pre_mix, norm_post_mix, norm_pre_mem, norm_mem_kv, norm_post_mem,
           norm_pre_ffn, norm_post_ffn, w_in, gm_v_norm, gm_w_s, gm_b_s, ssd_conv_w, ssd_conv_b,
           ssd_dt_bias, ssd_a_log, ssd_d, ssd_norm, w_branch_a, w_branch_b, w_out,
           xa_w_q, xa_w_kv, xa_w_o, ffn_w_up, ffn_conv_w, ffn_conv_b, ffn_w_down):
    batch, seq, _ = x.shape
    depth = w_in.shape[0]
    xf = x.reshape(batch * seq, D_MODEL)
    memf = mem.reshape(batch * MEM_LEN, D_MODEL)

    w_main = jnp.concatenate([w_in[:, :, :DT_OFF].astype(BF16),
                              w_in[:, :, DT_OFF + SSD_HEADS:].astype(BF16)], axis=2)
    w_dt = jnp.pad(w_in[:, :, DT_OFF:DT_OFF + SSD_HEADS].astype(BF16),
                   ((0, 0), (0, 0), (0, LANES - SSD_HEADS)))
    pad_heads = lambda v: _rows(jnp.pad(v, ((0, 0), (0, LANES - SSD_HEADS))))
    b_exp = jnp.repeat(jnp.swapaxes(gm_b_s, 1, 2), LANES, axis=2)
    d_exp = _rows(jnp.repeat(ssd_d, SSD_HEAD_DIM, axis=1))
    e_mat = (lax.broadcasted_iota(jnp.int32, (LANES, SSD_INNER), 0)
             == lax.broadcasted_iota(jnp.int32, (LANES, SSD_INNER), 1) // SSD_HEAD_DIM).astype(BF16)
    bf = lambda w: w.astype(BF16)
    w_a, w_b, w_o = bf(w_branch_a), bf(w_branch_b), bf(w_out)
    w_q, w_kv, w_xo = bf(xa_w_q), bf(xa_w_kv), bf(xa_w_o)
    w_up, w_down = bf(ffn_w_up), bf(ffn_w_down)

    for l in range(depth):
        proj, dt = _inproj(l, xf, _rows(norm_pre_mix), w_main, w_dt, _rows(gm_v_norm),
                           ssd_conv_w, _rows(ssd_conv_b), seq)
        ya, yb = _mixer(l, proj, dt, gm_w_s, b_exp, pad_heads(ssd_dt_bias), pad_heads(ssd_a_log),
                        d_exp, _rows(ssd_norm), e_mat, batch, seq)
        xf = _merge(l, ya, yb, proj, xf, w_a, w_b, w_o, _rows(norm_post_mix))

        kv = _kvproj(l, memf, _rows(norm_mem_kv), w_kv)
        xf = _xattn(l, xf, kv, _rows(norm_pre_mem), w_q, w_xo, _rows(norm_post_mem), seq)

        xf = _ffn(l, xf, _rows(norm_pre_ffn), w_up, ffn_conv_w, _rows(ffn_conv_b), w_down,
                  _rows(norm_post_ffn), seq)
    return xf.reshape(batch, seq, D_MODEL)
```

```python
import functools

import jax
import jax.numpy as jnp
from jax import lax
from jax.experimental import pallas as pl
from jax.experimental.pallas import tpu as pltpu

F32 = jnp.float32
BF16 = jnp.bfloat16

D_MODEL = 1024
MEM_LEN = 256
EPS = 1e-6
GM_GROUPS = 8
GM_WIDTH = 1024
CHUNK = 128
SSD_INNER = 2048
SSD_HEAD_DIM = 64
SSD_HEADS = 32
SSD_GROUPS = 4
SSD_HPG = 8
SSD_STATE = 128
SSD_CONV = 4
SSD_XBC = 3072
SSD_BC = 2 * SSD_GROUPS * SSD_STATE
GROUP_COLS = SSD_INNER // SSD_GROUPS
X_HEADS = 4
X_HEAD_DIM = 256
FFN_DIM = 2816
FFN_CONV = 3
LANES = 128
MXU_COLS = 256
TAIL_ROWS = 8
MAIN_COLS = 9216
DT_OFF = 7168
SEG_OFF = (0, 1024, 2048, 4096, 7168, 9216)
STAGE_DEPTH = 2
MIX_CHUNKS = 4
ROW_SUB = 512
FFN_DOWN_GROUP = 4

VMEM_LIMIT = 56 * 1024 * 1024


def _params(*sem):
    return pltpu.CompilerParams(dimension_semantics=sem, vmem_limit_bytes=VMEM_LIMIT)


def _layer_spec(layer, *shape, resident=False):
    kw = dict(pipeline_mode=pl.Buffered(1)) if resident else {}
    return pl.BlockSpec((None,) + shape, lambda *_: (layer,) + (0,) * len(shape), **kw)


def _dot(a, b):
    return jnp.dot(a, b, preferred_element_type=F32)


def _dot_nt(a, b):
    return lax.dot_general(a, b, (((1,), (1,)), ((), ())), preferred_element_type=F32)


def _rms(x, g):
    ms = jnp.mean(x * x, axis=-1, keepdims=True)
    return x * lax.rsqrt(ms + EPS) * g


GELU_C1 = 0.7978845608028654
GELU_C2 = GELU_C1 * 0.044715


def _gelu(x):
    half = 0.5 * x
    return half * jnp.tanh(x * (GELU_C1 + GELU_C2 * (x * x))) + half


def _sigmoid(x):
    return 0.5 * jnp.tanh(0.5 * x) + 0.5


def _silu(x):
    half = 0.5 * x
    return half * jnp.tanh(half) + half


def _softplus(x):
    return jnp.maximum(x, 0.0) + jnp.log1p(jnp.exp(-jnp.abs(x)))


def _causal_conv(cur, tail, w, b):
    k_width = w.shape[0]
    row = lax.broadcasted_iota(jnp.int32, tail.shape, 0)
    y = b + w[k_width - 1:k_width, :] * cur
    for s in range(1, k_width):
        r = pltpu.roll(cur, s, axis=0)
        rt = pltpu.roll(tail, s, axis=0)
        head = jnp.where(row < s, rt, r[0:TAIL_ROWS, :])
        r = jnp.concatenate([head, r[TAIL_ROWS:, :]], axis=0)
        y = y + w[k_width - 1 - s:k_width - s, :] * r
    return y


def _interleave(a, b):
    out = []
    for k in range(max(len(a), len(b))):
        out.extend(a[k:k + 1])
        out.extend(b[k:k + 1])
    return out


def _staged(n, produce, consume, bufs, slot0):
    assert len(bufs) == STAGE_DEPTH + 1
    for k in range(n + STAGE_DEPTH):
        if k < n:
            bufs[k % len(bufs)][slot0] = produce(k)
        if k >= STAGE_DEPTH:
            consume(k - STAGE_DEPTH, bufs[(k - STAGE_DEPTH) % len(bufs)][slot0])


def _inproj_kernel(tiles_per_seq, x_ref, g_ref, w_ref, wg_ref, wdt_ref, vgain_ref, cw_ref, cb_ref,
                   o_ref, dt_ref, h_scr, v_scr, tail_scr, *stage):
    i = pl.program_id(0)
    tm = x_ref.shape[0]
    sub = MXU_COLS
    h = _rms(x_ref[...], g_ref[...]).astype(BF16)
    h_scr[...] = h
    dt_ref[...] = _dot(h, wdt_ref[...])

    @pl.when(i % tiles_per_seq == 0)
    def _():
        tail_scr[...] = jnp.zeros_like(tail_scr)

    ssq = []

    def ep_u(raw, c0):
        o_ref[:, c0:c0 + sub] = _gelu(raw).astype(BF16)

    def ep_v(raw, c0):
        lc = c0 - SEG_OFF[1]
        gv = _gelu(raw)
        v_scr[:, lc:lc + sub] = gv
        ssq.append(jnp.sum(gv * gv, axis=-1, keepdims=True))

    def ep_z(raw, c0):
        o_ref[:, c0:c0 + sub] = _silu(raw).astype(BF16)

    def ep_xbc(raw, c0):
        lc = c0 - SEG_OFF[3]
        y = _causal_conv(raw, tail_scr[:, lc:lc + sub], cw_ref[:, lc:lc + sub], cb_ref[:, lc:lc + sub])
        tail_scr[:, lc:lc + sub] = raw[tm - TAIL_ROWS:, :]
        o_ref[:, c0:c0 + sub] = _silu(y).astype(BF16)

    def ep_gate(raw, c0):
        o_ref[:, c0:c0 + sub] = _sigmoid(raw).astype(BF16)

    def seg_tasks(*pairs):
        return [(fn, c0) for seg, fn in pairs for c0 in range(SEG_OFF[seg], SEG_OFF[seg + 1], sub)]

    tasks = _interleave(seg_tasks((3, ep_xbc), (1, ep_v), (0, ep_u)),
                        seg_tasks((2, ep_z), (4, ep_gate)))
    def produce(k):
        c0 = tasks[k][1]
        if c0 < SEG_OFF[4]:
            return _dot(h_scr[...], w_ref[:, c0:c0 + sub])
        return _dot(h_scr[...], wg_ref[:, c0 - SEG_OFF[4]:c0 - SEG_OFF[4] + sub])

    _staged(len(tasks), produce, lambda k, raw: tasks[k][0](raw, tasks[k][1]),
            stage, jnp.minimum(i, 0))

    scale = lax.rsqrt(sum(ssq) * (1.0 / GM_WIDTH) + EPS)
    for c0 in range(SEG_OFF[1], SEG_OFF[2], sub):
        lc = c0 - SEG_OFF[1]
        o_ref[:, c0:c0 + sub] = (v_scr[:, lc:lc + sub] * scale * vgain_ref[:, lc:lc + sub]).astype(BF16)


def _inproj(layer, x, g, w_front, w_gates, w_dt, v_gain, conv_w, conv_b, seq, tm=512):
    m = x.shape[0]
    return pl.pallas_call(
        functools.partial(_inproj_kernel, seq // tm),
        grid=(m // tm,),
        in_specs=[
            pl.BlockSpec((tm, D_MODEL), lambda i: (i, 0)),
            _layer_spec(layer, 1, D_MODEL),
            _layer_spec(layer, D_MODEL, SEG_OFF[4], resident=True),
            _layer_spec(layer, D_MODEL, MAIN_COLS - SEG_OFF[4], resident=True),
            _layer_spec(layer, D_MODEL, LANES, resident=True),
            _layer_spec(layer, 1, GM_WIDTH),
            _layer_spec(layer, SSD_CONV, SSD_XBC),
            _layer_spec(layer, 1, SSD_XBC),
        ],
        out_specs=[
            pl.BlockSpec((tm, MAIN_COLS), lambda i: (i, 0)),
            pl.BlockSpec((tm, LANES), lambda i: (i, 0)),
        ],
        out_shape=[
            jax.ShapeDtypeStruct((m, MAIN_COLS), BF16),
            jax.ShapeDtypeStruct((m, LANES), F32),
        ],
        scratch_shapes=[
            pltpu.VMEM((tm, D_MODEL), BF16),
            pltpu.VMEM((tm, GM_WIDTH), F32),
            pltpu.VMEM((TAIL_ROWS, SSD_XBC), F32),
        ] + [pltpu.VMEM((1, tm, MXU_COLS), F32)] * (STAGE_DEPTH + 1),
        compiler_params=_params("arbitrary"),
        name="inproj",
    )(x, g, w_front, w_gates, w_dt, v_gain, conv_w, conv_b)


def _kvproj_kernel(x_ref, g_ref, w_ref, o_ref):
    h = _rms(x_ref[...], g_ref[...]).astype(BF16)
    o_ref[...] = _dot(h, w_ref[...]).astype(BF16)


def _kvproj(layer, mem, g, w_kv, tm=512):
    m = mem.shape[0]
    n = 2 * D_MODEL
    return pl.pallas_call(
        _kvproj_kernel,
        grid=(m // tm,),
        in_specs=[
            pl.BlockSpec((tm, D_MODEL), lambda i: (i, 0)),
            _layer_spec(layer, 1, D_MODEL),
            _layer_spec(layer, D_MODEL, n),
        ],
        out_specs=pl.BlockSpec((tm, n), lambda i: (i, 0)),
        out_shape=jax.ShapeDtypeStruct((m, n), BF16),
        compiler_params=_params("arbitrary"),
        name="kvproj",
    )(mem, g, w_kv)


def _split2(x):
    hi = x.astype(BF16)
    lo = (x - hi.astype(F32)).astype(BF16)
    return hi, lo


def _split3(x):
    hi = x.astype(BF16)
    r = x - hi.astype(F32)
    mid = r.astype(BF16)
    lo = (r - mid.astype(F32)).astype(BF16)
    return hi, mid, lo


def _mixer_kernel(u_ref, v_ref, z_ref, xs_ref, bc_ref, dt_ref,
                  ws_ref, bexp_ref, dtb_ref, alog_ref, dexp_ref, ng_ref, e_ref,
                  ya_ref, yb_ref,
                  state_scr, wsm_scr):
    c = pl.program_id(1)
    L = CHUNK
    rowi = lax.broadcasted_iota(jnp.int32, (L, L), 0)
    coli = lax.broadcasted_iota(jnp.int32, (L, L), 1)
    causal = rowi >= coli
    lane = lax.broadcasted_iota(jnp.int32, (L, LANES), 1)
    head_lane = lane < SSD_HEADS
    first_half = lane < SSD_HEAD_DIM
    zero_bf = jnp.zeros((L, LANES), BF16)
    tril = jnp.where(causal, 1.0, 0.0).astype(BF16)
    a_row = -jnp.exp(alog_ref[...])

    @pl.when(c == 0)
    def _():
        state_scr[...] = jnp.zeros_like(state_scr)
        for g in range(GM_GROUPS):
            wsm_scr[g] = jnp.where(causal, ws_ref[g], 0.0).astype(BF16)

    for ck in range(u_ref.shape[0] // L):
        rows = pl.ds(ck * L, L)
        for g in range(GM_GROUPS):
            sl = slice(g * LANES, (g + 1) * LANES)
            mixed = _dot(wsm_scr[g], v_ref[rows, sl]) + bexp_ref[:, sl]
            ya_ref[rows, sl] = (u_ref[rows, sl].astype(F32) * mixed).astype(BF16)

        dt = jnp.where(head_lane, _softplus(dt_ref[rows, :] + dtb_ref[...]), 0.0)
        da = dt * a_row
        d_hi, d_mid, d_lo = _split3(da)
        a_cs = _dot(tril, d_hi) + _dot(tril, d_mid) + _dot(tril, d_lo)
        a_cs_t = a_cs.T
        dt_t = dt.T
        ea_hi, ea_lo = _split2(jnp.exp(a_cs))
        ws_hi, ws_lo = _split2(dt * jnp.exp(a_cs[L - 1:L, :] - a_cs))

        for g in range(SSD_GROUPS):
            gsl = slice(g * GROUP_COLS, (g + 1) * GROUP_COLS)
            bm_bf = bc_ref[rows, g * SSD_STATE:(g + 1) * SSD_STATE]
            cm_bf = bc_ref[rows, SSD_GROUPS * SSD_STATE + g * SSD_STATE:
                           SSD_GROUPS * SSD_STATE + (g + 1) * SSD_STATE]
            cb = _dot_nt(cm_bf, bm_bf)
            xs_g_bf = xs_ref[rows, gsl]
            xs_g = xs_g_bf.astype(F32)
            e_g = e_ref[:, gsl]
            ea_exp = _dot(ea_hi, e_g) + _dot(ea_lo, e_g)
            wst_exp = _dot(ws_hi, e_g) + _dot(ws_lo, e_g)
            state_g = state_scr[:, gsl]
            y_g = _dot(cm_bf, state_g.astype(BF16)) * ea_exp
            parts = []
            for pr in range(SSD_HPG // 2):
                sc = []
                for hh in range(2):
                    h = g * SSD_HPG + 2 * pr + hh
                    seg = a_cs[:, h:h + 1] - a_cs_t[h:h + 1, :]
                    lm = jnp.exp(jnp.where(causal, seg, -jnp.inf))
                    sc.append((cb * lm * dt_t[h:h + 1, :]).astype(BF16))
                lhs = jnp.concatenate(sc, axis=1)
                x_pair = xs_g_bf[:, pr * LANES:(pr + 1) * LANES]
                rhs = jnp.concatenate([jnp.where(first_half, x_pair, zero_bf),
                                       jnp.where(first_half, zero_bf, x_pair)], axis=0)
                parts.append(_dot(lhs, rhs))
            y_g = y_g + jnp.concatenate(parts, axis=1) + xs_g * dexp_ref[:, gsl]
            xw = (xs_g * wst_exp).astype(BF16)
            new_state = state_g * ea_exp[L - 1:L, :] + _dot(bm_bf.astype(F32).T.astype(BF16), xw)
            state_scr[:, gsl] = new_state
            y_g = y_g * z_ref[rows, gsl].astype(F32)
            yb_ref[rows, gsl] = _rms(y_g, ng_ref[:, gsl]).astype(BF16)


def _mixer(layer, proj, dt, w_s, b_exp, dt_bias, a_log, d_exp, norm_g, e_mat, batch, seq):
    rows = MIX_CHUNKS * CHUNK
    steps = seq // rows
    row = lambda b, c: b * steps + c
    in_specs = [
        pl.BlockSpec((rows, GM_WIDTH), lambda b, c: (row(b, c), 0)),
        pl.BlockSpec((rows, GM_WIDTH), lambda b, c: (row(b, c), 1)),
        pl.BlockSpec((rows, SSD_INNER), lambda b, c: (row(b, c), 1)),
        pl.BlockSpec((rows, SSD_INNER), lambda b, c: (row(b, c), 2)),
        pl.BlockSpec((rows, SSD_BC), lambda b, c: (row(b, c), 6)),
        pl.BlockSpec((rows, LANES), lambda b, c: (row(b, c), 0)),
        _layer_spec(layer, GM_GROUPS, CHUNK, CHUNK),
        _layer_spec(layer, CHUNK, GM_WIDTH),
        _layer_spec(layer, 1, LANES),
        _layer_spec(layer, 1, LANES),
        _layer_spec(layer, 1, SSD_INNER),
        _layer_spec(layer, 1, SSD_INNER),
        pl.BlockSpec((LANES, SSD_INNER), lambda b, c: (0, 0)),
    ]
    m = batch * seq
    return pl.pallas_call(
        _mixer_kernel,
        grid=(batch, steps),
        in_specs=in_specs,
        out_specs=[
            pl.BlockSpec((rows, GM_WIDTH), lambda b, c: (row(b, c), 0)),
            pl.BlockSpec((rows, SSD_INNER), lambda b, c: (row(b, c), 0)),
        ],
        out_shape=[
            jax.ShapeDtypeStruct((m, GM_WIDTH), BF16),
            jax.ShapeDtypeStruct((m, SSD_INNER), BF16),
        ],
        scratch_shapes=[
            pltpu.VMEM((SSD_STATE, SSD_INNER), F32),
            pltpu.VMEM((GM_GROUPS, CHUNK, CHUNK), BF16),
        ],
        compiler_params=_params("arbitrary", "arbitrary"),
        name="mixer",
    )(proj, proj, proj, proj, proj, dt, w_s, b_exp, dt_bias, a_log, d_exp, norm_g, e_mat)


def _merge_kernel(ya_ref, yb_ref, ga_ref, gb_ref, x_ref, wa_ref, wb_ref, wo_ref, g_ref, o_ref):
    for r0 in range(0, x_ref.shape[0], ROW_SUB):
        rows = pl.ds(r0, ROW_SUB)
        ya = _dot(ya_ref[rows, :], wa_ref[...])
        yb = _dot(yb_ref[rows, :], wb_ref[...])
        mixed = ga_ref[rows, :].astype(F32) * ya + gb_ref[rows, :].astype(F32) * yb
        y = _dot(mixed.astype(BF16), wo_ref[...])
        o_ref[rows, :] = x_ref[rows, :] + _rms(y, g_ref[...])


def _merge(layer, ya, yb, proj, x, w_a, w_b, w_o, g, tm=1024):
    m = x.shape[0]
    return pl.pallas_call(
        _merge_kernel,
        grid=(m // tm,),
        in_specs=[
            pl.BlockSpec((tm, GM_WIDTH), lambda i: (i, 0)),
            pl.BlockSpec((tm, SSD_INNER), lambda i: (i, 0)),
            pl.BlockSpec((tm, D_MODEL), lambda i: (i, 7)),
            pl.BlockSpec((tm, D_MODEL), lambda i: (i, 8)),
            pl.BlockSpec((tm, D_MODEL), lambda i: (i, 0)),
            _layer_spec(layer, GM_WIDTH, D_MODEL, resident=True),
            _layer_spec(layer, SSD_INNER, D_MODEL, resident=True),
            _layer_spec(layer, D_MODEL, D_MODEL, resident=True),
            _layer_spec(layer, 1, D_MODEL),
        ],
        out_specs=pl.BlockSpec((tm, D_MODEL), lambda i: (i, 0)),
        out_shape=jax.ShapeDtypeStruct((m, D_MODEL), F32),
        compiler_params=_params("arbitrary"),
        name="merge",
    )(ya, yb, proj, proj, x, w_a, w_b, w_o, g)


def _xattn_kernel(x_ref, gpre_ref, wq_ref, k_ref, v_ref, wo_ref, gpost_ref, o_ref):
    for r0 in range(0, x_ref.shape[0], ROW_SUB):
        rows = pl.ds(r0, ROW_SUB)
        x = x_ref[rows, :]
        h = _rms(x, gpre_ref[...]).astype(BF16)
        q = (_dot(h, wq_ref[...]) * (X_HEAD_DIM ** -0.5)).astype(BF16)
        outs = []
        for hd in range(X_HEADS):
            sl = slice(hd * X_HEAD_DIM, (hd + 1) * X_HEAD_DIM)
            s = _dot_nt(q[:, sl], k_ref[:, sl])
            e = jnp.exp(s - jnp.max(s, axis=-1, keepdims=True))
            p = e / jnp.sum(e, axis=-1, keepdims=True)
            outs.append(_dot(p.astype(BF16), v_ref[:, sl]).astype(BF16))
        o = jnp.concatenate(outs, axis=1)
        y = _dot(o, wo_ref[...])
        o_ref[rows, :] = x + _rms(y, gpost_ref[...])


def _xattn(layer, x, kv, g_pre, w_q, w_o, g_post, seq, tm=2048):
    m = x.shape[0]
    tiles = seq // tm
    return pl.pallas_call(
        _xattn_kernel,
        grid=(m // tm,),
        in_specs=[
            pl.BlockSpec((tm, D_MODEL), lambda i: (i, 0)),
            _layer_spec(layer, 1, D_MODEL),
            _layer_spec(layer, D_MODEL, D_MODEL),
            pl.BlockSpec((MEM_LEN, D_MODEL), lambda i: (i // tiles, 0)),
            pl.BlockSpec((MEM_LEN, D_MODEL), lambda i: (i // tiles, 1)),
            _layer_spec(layer, D_MODEL, D_MODEL),
            _layer_spec(layer, 1, D_MODEL),
        ],
        out_specs=pl.BlockSpec((tm, D_MODEL), lambda i: (i, 0)),
        out_shape=jax.ShapeDtypeStruct((m, D_MODEL), F32),
        compiler_params=_params("arbitrary"),
        name="xattn",
    )(x, g_pre, w_q, kv, kv, w_o, g_post)


def _ffn_kernel(tiles_per_seq, x_ref, gpre_ref, wup_ref, cw_ref, cb_ref, wd_ref, gpost_ref, o_ref,
                h_scr, g_scr, tail_scr, *stage):
    i = pl.program_id(0)
    tm = x_ref.shape[0]
    sub = MXU_COLS
    h_scr[...] = _rms(x_ref[...], gpre_ref[...]).astype(BF16)

    @pl.when(i % tiles_per_seq == 0)
    def _():
        tail_scr[...] = jnp.zeros_like(tail_scr)

    def produce(k):
        c0 = (k // 2) * sub + (k % 2) * FFN_DIM
        return _dot(h_scr[...], wup_ref[:, c0:c0 + sub])

    pending = {}
    down = []
    n_sub = FFN_DIM // sub

    def consume(k, raw):
        c0 = (k // 2) * sub + (k % 2) * FFN_DIM
        y = _causal_conv(raw, tail_scr[:, c0:c0 + sub], cw_ref[:, c0:c0 + sub], cb_ref[:, c0:c0 + sub])
        tail_scr[:, c0:c0 + sub] = raw[tm - TAIL_ROWS:, :]
        if k % 2 == 0:
            pending[k // 2] = _gelu(y)
            return
        j = k // 2
        g_scr[:, j * sub:(j + 1) * sub] = (pending.pop(j) * y).astype(BF16)
        if (j + 1) % FFN_DOWN_GROUP == 0 or j == n_sub - 1:
            lo = (j // FFN_DOWN_GROUP) * FFN_DOWN_GROUP * sub
            down.append(_dot(g_scr[:, lo:(j + 1) * sub], wd_ref[lo:(j + 1) * sub, :]))

    _staged(2 * n_sub, produce, consume, stage, jnp.minimum(i, 0))

    o_ref[...] = x_ref[...] + _rms(sum(down), gpost_ref[...])


def _ffn(layer, x, g_pre, w_up, conv_w, conv_b, w_down, g_post, seq, tm=512):
    m = x.shape[0]
    return pl.pallas_call(
        functools.partial(_ffn_kernel, seq // tm),
        grid=(m // tm,),
        in_specs=[
            pl.BlockSpec((tm, D_MODEL), lambda i: (i, 0)),
            _layer_spec(layer, 1, D_MODEL),
            _layer_spec(layer, D_MODEL, 2 * FFN_DIM, resident=True),
            _layer_spec(layer, FFN_CONV, 2 * FFN_DIM),
            _layer_spec(layer, 1, 2 * FFN_DIM),
            _layer_spec(layer, FFN_DIM, D_MODEL, resident=True),
            _layer_spec(layer, 1, D_MODEL),
        ],
        out_specs=pl.BlockSpec((tm, D_MODEL), lambda i: (i, 0)),
        out_shape=jax.ShapeDtypeStruct((m, D_MODEL), F32),
        scratch_shapes=[
            pltpu.VMEM((tm, D_MODEL), BF16),
            pltpu.VMEM((tm, FFN_DIM), BF16),
            pltpu.VMEM((TAIL_ROWS, 2 * FFN_DIM), F32),
        ] + [pltpu.VMEM((1, tm, MXU_COLS), F32)] * (STAGE_DEPTH + 1),
        compiler_params=_params("arbitrary"),
        name="ffn",
    )(x, g_pre, w_up, conv_w, conv_b, w_down, g_post)


def _rows(v):
    return v.reshape(v.shape[0], 1, v.shape[1])


def kernel(x, mem, norm_pre_mix, norm_post_mix, norm_pre_mem, norm_mem_kv, norm_post_mem,
           norm_pre_ffn, norm_post_ffn, w_in, gm_v_norm, gm_w_s, gm_b_s, ssd_conv_w, ssd_conv_b,
           ssd_dt_bias, ssd_a_log, ssd_d, ssd_norm, w_branch_a, w_branch_b, w_out,
           xa_w_q, xa_w_kv, xa_w_o, ffn_w_up, ffn_conv_w, ffn_conv_b, ffn_w_down):
    batch, seq, _ = x.shape
    depth = w_in.shape[0]
    xf = x.reshape(batch * seq, D_MODEL)
    memf = mem.reshape(batch * MEM_LEN, D_MODEL)

    w_front = w_in[:, :, :DT_OFF].astype(BF16)
    w_gates = w_in[:, :, DT_OFF + SSD_HEADS:].astype(BF16)
    w_dt = jnp.pad(w_in[:, :, DT_OFF:DT_OFF + SSD_HEADS].astype(BF16),
                   ((0, 0), (0, 0), (0, LANES - SSD_HEADS)))
    pad_heads = lambda v: _rows(jnp.pad(v, ((0, 0), (0, LANES - SSD_HEADS))))
    b_exp = jnp.repeat(jnp.swapaxes(gm_b_s, 1, 2), LANES, axis=2)
    d_exp = _rows(jnp.repeat(ssd_d, SSD_HEAD_DIM, axis=1))
    e_mat = (lax.broadcasted_iota(jnp.int32, (LANES, SSD_INNER), 0)
             == lax.broadcasted_iota(jnp.int32, (LANES, SSD_INNER), 1) // SSD_HEAD_DIM).astype(BF16)
    bf = lambda w: w.astype(BF16)
    w_a, w_b, w_o = bf(w_branch_a), bf(w_branch_b), bf(w_out)
    w_q, w_kv, w_xo = bf(xa_w_q), bf(xa_w_kv), bf(xa_w_o)
    w_up, w_down = bf(ffn_w_up), bf(ffn_w_down)

    for l in range(depth):
        proj, dt = _inproj(l, xf, _rows(norm_pre_mix), w_front, w_gates, w_dt, _rows(gm_v_norm),
                           ssd_conv_w, _rows(ssd_conv_b), seq)
        ya, yb = _mixer(l, proj, dt, gm_w_s, b_exp, pad_heads(ssd_dt_bias), pad_heads(ssd_a_log),
                        d_exp, _rows(ssd_norm), e_mat, batch, seq)
        xf = _merge(l, ya, yb, proj, xf, w_a, w_b, w_o, _rows(norm_post_mix))

        kv = _kvproj(l, memf, _rows(norm_mem_kv), w_kv)
        xf = _xattn(l, xf, kv, _rows(norm_pre_mem), w_q, w_xo, _rows(norm_post_mem), seq)

        xf = _ffn(l, xf, _rows(norm_pre_ffn), w_up, ffn_conv_w, _rows(ffn_conv_b), w_down,
                  _rows(norm_post_ffn), seq)
    return xf.reshape(batch, seq, D_MODEL)
```

```python
import functools

import jax
import jax.numpy as jnp
from jax import lax
from jax.experimental import pallas as pl
from jax.experimental.pallas import tpu as pltpu

F32 = jnp.float32
BF16 = jnp.bfloat16

D_MODEL = 1024
MEM_LEN = 256
EPS = 1e-6
GM_GROUPS = 8
GM_WIDTH = 1024
CHUNK = 128
SSD_INNER = 2048
SSD_HEAD_DIM = 64
SSD_HEADS = 32
SSD_GROUPS = 4
SSD_HPG = 8
SSD_STATE = 128
SSD_CONV = 4
SSD_XBC = 3072
SSD_BC = 2 * SSD_GROUPS * SSD_STATE
GROUP_COLS = SSD_INNER // SSD_GROUPS
X_HEADS = 4
X_HEAD_DIM = 256
FFN_DIM = 2816
FFN_CONV = 3
LANES = 128
MXU_COLS = 256
TAIL_ROWS = 8
MAIN_COLS = 9216
DT_OFF = 7168
SEG_OFF = (0, 1024, 2048, 4096, 7168, 9216)
STAGE_DEPTH = 2
MIX_CHUNKS = 4
ROW_SUB = 512
FFN_DOWN_GROUP = 4

VMEM_LIMIT = 56 * 1024 * 1024


def _params(*sem):
    return pltpu.CompilerParams(dimension_semantics=sem, vmem_limit_bytes=VMEM_LIMIT)


def _layer_spec(layer, *shape, resident=False):
    kw = dict(pipeline_mode=pl.Buffered(1)) if resident else {}
    return pl.BlockSpec((None,) + shape, lambda *_: (layer,) + (0,) * len(shape), **kw)


def _dot(a, b):
    return jnp.dot(a, b, preferred_element_type=F32)


def _dot_nt(a, b):
    return lax.dot_general(a, b, (((1,), (1,)), ((), ())), preferred_element_type=F32)


def _rms(x, g):
    ms = jnp.mean(x * x, axis=-1, keepdims=True)
    return x * lax.rsqrt(ms + EPS) * g


GELU_C1 = 0.7978845608028654
GELU_C2 = GELU_C1 * 0.044715


def _gelu(x):
    half = 0.5 * x
    return half * jnp.tanh(x * (GELU_C1 + GELU_C2 * (x * x))) + half


def _sigmoid(x):
    return 0.5 * jnp.tanh(0.5 * x) + 0.5


def _silu(x):
    half = 0.5 * x
    return half * jnp.tanh(half) + half


def _softplus(x):
    return jnp.maximum(x, 0.0) + jnp.log1p(jnp.exp(-jnp.abs(x)))


def _causal_conv(cur, tail, w, b):
    k_width = w.shape[0]
    row = lax.broadcasted_iota(jnp.int32, tail.shape, 0)
    y = b + w[k_width - 1:k_width, :] * cur
    for s in range(1, k_width):
        r = pltpu.roll(cur, s, axis=0)
        rt = pltpu.roll(tail, s, axis=0)
        head = jnp.where(row < s, rt, r[0:TAIL_ROWS, :])
        r = jnp.concatenate([head, r[TAIL_ROWS:, :]], axis=0)
        y = y + w[k_width - 1 - s:k_width - s, :] * r
    return y


def _interleave(a, b):
    out = []
    for k in range(max(len(a), len(b))):
        out.extend(a[k:k + 1])
        out.extend(b[k:k + 1])
    return out


def _staged(n, produce, consume, bufs, slot0):
    assert len(bufs) == STAGE_DEPTH + 1
    for k in range(n + STAGE_DEPTH):
        if k < n:
            bufs[k % len(bufs)][slot0] = produce(k)
        if k >= STAGE_DEPTH:
            consume(k - STAGE_DEPTH, bufs[(k - STAGE_DEPTH) % len(bufs)][slot0])


def _inproj_kernel(tiles_per_seq, x_ref, g_ref, w_ref, wg_ref, wdt_ref, vgain_ref, cw_ref, cb_ref,
                   o_ref, dt_ref, h_scr, v_scr, tail_scr, *stage):
    i = pl.program_id(0)
    tm = x_ref.shape[0]
    sub = MXU_COLS
    h = _rms(x_ref[...], g_ref[...]).astype(BF16)
    h_scr[...] = h
    dt_ref[...] = _dot(h, wdt_ref[...])

    @pl.when(i % tiles_per_seq == 0)
    def _():
        tail_scr[...] = jnp.zeros_like(tail_scr)

    ssq = []

    def ep_u(raw, c0):
        o_ref[:, c0:c0 + sub] = _gelu(raw).astype(BF16)

    def ep_v(raw, c0):
        lc = c0 - SEG_OFF[1]
        gv = _gelu(raw)
        v_scr[:, lc:lc + sub] = gv
        ssq.append(jnp.sum(gv * gv, axis=-1, keepdims=True))
        if c0 + sub == SEG_OFF[2]:
            scale = lax.rsqrt(sum(ssq) * (1.0 / GM_WIDTH) + EPS)
            for d0 in range(0, GM_WIDTH, sub):
                o_ref[:, SEG_OFF[1] + d0:SEG_OFF[1] + d0 + sub] = (
                    v_scr[:, d0:d0 + sub] * scale * vgain_ref[:, d0:d0 + sub]).astype(BF16)

    def ep_z(raw, c0):
        o_ref[:, c0:c0 + sub] = _silu(raw).astype(BF16)

    def ep_xbc(raw, c0):
        lc = c0 - SEG_OFF[3]
        y = _causal_conv(raw, tail_scr[:, lc:lc + sub], cw_ref[:, lc:lc + sub], cb_ref[:, lc:lc + sub])
        tail_scr[:, lc:lc + sub] = raw[tm - TAIL_ROWS:, :]
        o_ref[:, c0:c0 + sub] = _silu(y).astype(BF16)

    def ep_gate(raw, c0):
        o_ref[:, c0:c0 + sub] = _sigmoid(raw).astype(BF16)

    def seg_tasks(*pairs):
        return [(fn, c0) for seg, fn in pairs for c0 in range(SEG_OFF[seg], SEG_OFF[seg + 1], sub)]

    tasks = seg_tasks((1, ep_v)) + _interleave(seg_tasks((3, ep_xbc), (0, ep_u)),
                                               seg_tasks((2, ep_z), (4, ep_gate)))

    def produce(k):
        c0 = tasks[k][1]
        if c0 < SEG_OFF[4]:
            return _dot(h_scr[...], w_ref[:, c0:c0 + sub])
        return _dot(h_scr[...], wg_ref[:, c0 - SEG_OFF[4]:c0 - SEG_OFF[4] + sub])

    _staged(len(tasks), produce, lambda k, raw: tasks[k][0](raw, tasks[k][1]),
            stage, jnp.minimum(i, 0))


def _inproj(layer, x, g, w_front, w_gates, w_dt, v_gain, conv_w, conv_b, seq, tm=512):
    m = x.shape[0]
    return pl.pallas_call(
        functools.partial(_inproj_kernel, seq // tm),
        grid=(m // tm,),
        in_specs=[
            pl.BlockSpec((tm, D_MODEL), lambda i: (i, 0)),
            _layer_spec(layer, 1, D_MODEL),
            _layer_spec(layer, D_MODEL, SEG_OFF[4], resident=True),
            _layer_spec(layer, D_MODEL, MAIN_COLS - SEG_OFF[4], resident=True),
            _layer_spec(layer, D_MODEL, LANES, resident=True),
            _layer_spec(layer, 1, GM_WIDTH),
            _layer_spec(layer, SSD_CONV, SSD_XBC),
            _layer_spec(layer, 1, SSD_XBC),
        ],
        out_specs=[
            pl.BlockSpec((tm, MAIN_COLS), lambda i: (i, 0)),
            pl.BlockSpec((tm, LANES), lambda i: (i, 0)),
        ],
        out_shape=[
            jax.ShapeDtypeStruct((m, MAIN_COLS), BF16),
            jax.ShapeDtypeStruct((m, LANES), F32),
        ],
        scratch_shapes=[
            pltpu.VMEM((tm, D_MODEL), BF16),
            pltpu.VMEM((tm, GM_WIDTH), F32),
            pltpu.VMEM((TAIL_ROWS, SSD_XBC), F32),
        ] + [pltpu.VMEM((1, tm, MXU_COLS), F32)] * (STAGE_DEPTH + 1),
        compiler_params=_params("arbitrary"),
        name="inproj",
    )(x, g, w_front, w_gates, w_dt, v_gain, conv_w, conv_b)


def _kvproj_kernel(x_ref, g_ref, w_ref, o_ref):
    h = _rms(x_ref[...], g_ref[...]).astype(BF16)
    o_ref[...] = _dot(h, w_ref[...]).astype(BF16)


def _kvproj(layer, mem, g, w_kv, tm=512):
    m = mem.shape[0]
    n = 2 * D_MODEL
    return pl.pallas_call(
        _kvproj_kernel,
        grid=(m // tm,),
        in_specs=[
            pl.BlockSpec((tm, D_MODEL), lambda i: (i, 0)),
            _layer_spec(layer, 1, D_MODEL),
            _layer_spec(layer, D_MODEL, n),
        ],
        out_specs=pl.BlockSpec((tm, n), lambda i: (i, 0)),
        out_shape=jax.ShapeDtypeStruct((m, n), BF16),
        compiler_params=_params("arbitrary"),
        name="kvproj",
    )(mem, g, w_kv)


def _split2(x):
    hi = x.astype(BF16)
    lo = (x - hi.astype(F32)).astype(BF16)
    return hi, lo


def _split3(x):
    hi = x.astype(BF16)
    r = x - hi.astype(F32)
    mid = r.astype(BF16)
    lo = (r - mid.astype(F32)).astype(BF16)
    return hi, mid, lo


def _mixer_kernel(u_ref, v_ref, z_ref, xs_ref, bc_ref, dt_ref,
                  ws_ref, bexp_ref, dtb_ref, alog_ref, dexp_ref, ng_ref, e_ref,
                  ya_ref, yb_ref,
                  state_scr, wsm_scr):
    c = pl.program_id(1)
    L = CHUNK
    rowi = lax.broadcasted_iota(jnp.int32, (L, L), 0)
    coli = lax.broadcasted_iota(jnp.int32, (L, L), 1)
    causal = rowi >= coli
    lane = lax.broadcasted_iota(jnp.int32, (L, LANES), 1)
    head_lane = lane < SSD_HEADS
    first_half = lane < SSD_HEAD_DIM
    zero_bf = jnp.zeros((L, LANES), BF16)
    tril = jnp.where(causal, 1.0, 0.0).astype(BF16)
    a_row = -jnp.exp(alog_ref[...])

    @pl.when(c == 0)
    def _():
        state_scr[...] = jnp.zeros_like(state_scr)
        for g in range(GM_GROUPS):
            wsm_scr[g] = jnp.where(causal, ws_ref[g], 0.0).astype(BF16)

    for ck in range(u_ref.shape[0] // L):
        rows = pl.ds(ck * L, L)
        for g in range(GM_GROUPS):
            sl = slice(g * LANES, (g + 1) * LANES)
            mixed = _dot(wsm_scr[g], v_ref[rows, sl]) + bexp_ref[:, sl]
            ya_ref[rows, sl] = (u_ref[rows, sl].astype(F32) * mixed).astype(BF16)

        dt = jnp.where(head_lane, _softplus(dt_ref[rows, :] + dtb_ref[...]), 0.0)
        da = dt * a_row
        d_hi, d_mid, d_lo = _split3(da)
        a_cs = _dot(tril, d_hi) + _dot(tril, d_mid) + _dot(tril, d_lo)
        a_cs_t = a_cs.T
        dt_t = dt.T
        ea2 = jnp.concatenate(_split2(jnp.exp(a_cs)), axis=1)
        ws2 = jnp.concatenate(_split2(dt * jnp.exp(a_cs[L - 1:L, :] - a_cs)), axis=1)

        for g in range(SSD_GROUPS):
            gsl = slice(g * GROUP_COLS, (g + 1) * GROUP_COLS)
            bm_bf = bc_ref[rows, g * SSD_STATE:(g + 1) * SSD_STATE]
            cm_bf = bc_ref[rows, SSD_GROUPS * SSD_STATE + g * SSD_STATE:
                           SSD_GROUPS * SSD_STATE + (g + 1) * SSD_STATE]
            cb = _dot_nt(cm_bf, bm_bf)
            xs_g_bf = xs_ref[rows, gsl]
            xs_g = xs_g_bf.astype(F32)
            e_g = e_ref[:, gsl]
            ea_exp = _dot(ea2, e_g)
            wst_exp = _dot(ws2, e_g)
            state_g = state_scr[:, gsl]
            y_g = _dot(cm_bf, state_g.astype(BF16)) * ea_exp
            parts = []
            for pr in range(SSD_HPG // 2):
                sc = []
                for hh in range(2):
                    h = g * SSD_HPG + 2 * pr + hh
                    seg = a_cs[:, h:h + 1] - a_cs_t[h:h + 1, :]
                    lm = jnp.exp(jnp.where(causal, seg, -jnp.inf))
                    sc.append((cb * lm * dt_t[h:h + 1, :]).astype(BF16))
                lhs = jnp.concatenate(sc, axis=1)
                x_pair = xs_g_bf[:, pr * LANES:(pr + 1) * LANES]
                rhs = jnp.concatenate([jnp.where(first_half, x_pair, zero_bf),
                                       jnp.where(first_half, zero_bf, x_pair)], axis=0)
                parts.append(_dot(lhs, rhs))
            y_g = y_g + jnp.concatenate(parts, axis=1) + xs_g * dexp_ref[:, gsl]
            xw = (xs_g * wst_exp).astype(BF16)
            new_state = state_g * ea_exp[L - 1:L, :] + _dot(bm_bf.astype(F32).T.astype(BF16), xw)
            state_scr[:, gsl] = new_state
            y_g = y_g * z_ref[rows, gsl].astype(F32)
            yb_ref[rows, gsl] = _rms(y_g, ng_ref[:, gsl]).astype(BF16)


def _mixer(layer, proj, dt, w_s, b_exp, dt_bias, a_log, d_exp, norm_g, e_mat, batch, seq):
    rows = MIX_CHUNKS * CHUNK
    steps = seq // rows
    row = lambda b, c: b * steps + c
    in_specs = [
        pl.BlockSpec((rows, GM_WIDTH), lambda b, c: (row(b, c), 0)),
        pl.BlockSpec((rows, GM_WIDTH), lambda b, c: (row(b, c), 1)),
        pl.BlockSpec((rows, SSD_INNER), lambda b, c: (row(b, c), 1)),
        pl.BlockSpec((rows, SSD_INNER), lambda b, c: (row(b, c), 2)),
        pl.BlockSpec((rows, SSD_BC), lambda b, c: (row(b, c), 6)),
        pl.BlockSpec((rows, LANES), lambda b, c: (row(b, c), 0)),
        _layer_spec(layer, GM_GROUPS, CHUNK, CHUNK),
        _layer_spec(layer, CHUNK, GM_WIDTH),
        _layer_spec(layer, 1, LANES),
        _layer_spec(layer, 1, LANES),
        _layer_spec(layer, 1, SSD_INNER),
        _layer_spec(layer, 1, SSD_INNER),
        pl.BlockSpec((2 * LANES, SSD_INNER), lambda b, c: (0, 0)),
    ]
    m = batch * seq
    return pl.pallas_call(
        _mixer_kernel,
        grid=(batch, steps),
        in_specs=in_specs,
        out_specs=[
            pl.BlockSpec((rows, GM_WIDTH), lambda b, c: (row(b, c), 0)),
            pl.BlockSpec((rows, SSD_INNER), lambda b, c: (row(b, c), 0)),
        ],
        out_shape=[
            jax.ShapeDtypeStruct((m, GM_WIDTH), BF16),
            jax.ShapeDtypeStruct((m, SSD_INNER), BF16),
        ],
        scratch_shapes=[
            pltpu.VMEM((SSD_STATE, SSD_INNER), F32),
            pltpu.VMEM((GM_GROUPS, CHUNK, CHUNK), BF16),
        ],
        compiler_params=_params("arbitrary", "arbitrary"),
        name="mixer",
    )(proj, proj, proj, proj, proj, dt, w_s, b_exp, dt_bias, a_log, d_exp, norm_g, e_mat)


def _merge_kernel(ya_ref, yb_ref, ga_ref, gb_ref, x_ref, wa_ref, wb_ref, wo_ref, g_ref, o_ref):
    for r0 in range(0, x_ref.shape[0], ROW_SUB):
        rows = pl.ds(r0, ROW_SUB)
        ya = _dot(ya_ref[rows, :], wa_ref[...])
        yb = _dot(yb_ref[rows, :], wb_ref[...])
        mixed = ga_ref[rows, :].astype(F32) * ya + gb_ref[rows, :].astype(F32) * yb
        y = _dot(mixed.astype(BF16), wo_ref[...])
        o_ref[rows, :] = x_ref[rows, :] + _rms(y, g_ref[...])


def _merge(layer, ya, yb, proj, x, w_a, w_b, w_o, g, tm=1024):
    m = x.shape[0]
    return pl.pallas_call(
        _merge_kernel,
        grid=(m // tm,),
        in_specs=[
            pl.BlockSpec((tm, GM_WIDTH), lambda i: (i, 0)),
            pl.BlockSpec((tm, SSD_INNER), lambda i: (i, 0)),
            pl.BlockSpec((tm, D_MODEL), lambda i: (i, 7)),
            pl.BlockSpec((tm, D_MODEL), lambda i: (i, 8)),
            pl.BlockSpec((tm, D_MODEL), lambda i: (i, 0)),
            _layer_spec(layer, GM_WIDTH, D_MODEL, resident=True),
            _layer_spec(layer, SSD_INNER, D_MODEL, resident=True),
            _layer_spec(layer, D_MODEL, D_MODEL, resident=True),
            _layer_spec(layer, 1, D_MODEL),
        ],
        out_specs=pl.BlockSpec((tm, D_MODEL), lambda i: (i, 0)),
        out_shape=jax.ShapeDtypeStruct((m, D_MODEL), F32),
        compiler_params=_params("arbitrary"),
        name="merge",
    )(ya, yb, proj, proj, x, w_a, w_b, w_o, g)


def _xattn_kernel(x_ref, gpre_ref, wq_ref, k_ref, v_ref, wo_ref, gpost_ref, o_ref):
    for r0 in range(0, x_ref.shape[0], ROW_SUB):
        rows = pl.ds(r0, ROW_SUB)
        x = x_ref[rows, :]
        h = _rms(x, gpre_ref[...]).astype(BF16)
        q = (_dot(h, wq_ref[...]) * (X_HEAD_DIM ** -0.5)).astype(BF16)
        outs = []
        for hd in range(X_HEADS):
            sl = slice(hd * X_HEAD_DIM, (hd + 1) * X_HEAD_DIM)
            s = _dot_nt(q[:, sl], k_ref[:, sl])
            e = jnp.exp(s - jnp.max(s, axis=-1, keepdims=True))
            p = e / jnp.sum(e, axis=-1, keepdims=True)
            outs.append(_dot(p.astype(BF16), v_ref[:, sl]).astype(BF16))
        o = jnp.concatenate(outs, axis=1)
        y = _dot(o, wo_ref[...])
        o_ref[rows, :] = x + _rms(y, gpost_ref[...])


def _xattn(layer, x, kv, g_pre, w_q, w_o, g_post, seq, tm=2048):
    m = x.shape[0]
    tiles = seq // tm
    return pl.pallas_call(
        _xattn_kernel,
        grid=(m // tm,),
        in_specs=[
            pl.BlockSpec((tm, D_MODEL), lambda i: (i, 0)),
            _layer_spec(layer, 1, D_MODEL),
            _layer_spec(layer, D_MODEL, D_MODEL),
            pl.BlockSpec((MEM_LEN, D_MODEL), lambda i: (i // tiles, 0)),
            pl.BlockSpec((MEM_LEN, D_MODEL), lambda i: (i // tiles, 1)),
            _layer_spec(layer, D_MODEL, D_MODEL),
            _layer_spec(layer, 1, D_MODEL),
        ],
        out_specs=pl.BlockSpec((tm, D_MODEL), lambda i: (i, 0)),
        out_shape=jax.ShapeDtypeStruct((m, D_MODEL), F32),
        compiler_params=_params("arbitrary"),
        name="xattn",
    )(x, g_pre, w_q, kv, kv, w_o, g_post)


def _ffn_kernel(tiles_per_seq, x_ref, gpre_ref, wup_ref, cw_ref, cb_ref, wd_ref, gpost_ref, o_ref,
                h_scr, g_scr, tail_scr, *stage):
    i = pl.program_id(0)
    tm = x_ref.shape[0]
    sub = MXU_COLS
    h_scr[...] = _rms(x_ref[...], gpre_ref[...]).astype(BF16)

    @pl.when(i % tiles_per_seq == 0)
    def _():
        tail_scr[...] = jnp.zeros_like(tail_scr)

    def produce(k):
        c0 = (k // 2) * sub + (k % 2) * FFN_DIM
        return _dot(h_scr[...], wup_ref[:, c0:c0 + sub])

    pending = {}
    down = []
    n_sub = FFN_DIM // sub

    def consume(k, raw):
        c0 = (k // 2) * sub + (k % 2) * FFN_DIM
        y = _causal_conv(raw, tail_scr[:, c0:c0 + sub], cw_ref[:, c0:c0 + sub], cb_ref[:, c0:c0 + sub])
        tail_scr[:, c0:c0 + sub] = raw[tm - TAIL_ROWS:, :]
        if k % 2 == 0:
            pending[k // 2] = _gelu(y)
            return
        j = k // 2
        g_scr[:, j * sub:(j + 1) * sub] = (pending.pop(j) * y).astype(BF16)
        if (j + 1) % FFN_DOWN_GROUP == 0 or j == n_sub - 1:
            lo = (j // FFN_DOWN_GROUP) * FFN_DOWN_GROUP * sub
            down.append(_dot(g_scr[:, lo:(j + 1) * sub], wd_ref[lo:(j + 1) * sub, :]))

    _staged(2 * n_sub, produce, consume, stage, jnp.minimum(i, 0))

    o_ref[...] = x_ref[...] + _rms(sum(down), gpost_ref[...])


def _ffn(layer, x, g_pre, w_up, conv_w, conv_b, w_down, g_post, seq, tm=512):
    m = x.shape[0]
    return pl.pallas_call(
        functools.partial(_ffn_kernel, seq // tm),
        grid=(m // tm,),
        in_specs=[
            pl.BlockSpec((tm, D_MODEL), lambda i: (i, 0)),
            _layer_spec(layer, 1, D_MODEL),
            _layer_spec(layer, D_MODEL, 2 * FFN_DIM, resident=True),
            _layer_spec(layer, FFN_CONV, 2 * FFN_DIM),
            _layer_spec(layer, 1, 2 * FFN_DIM),
            _layer_spec(layer, FFN_DIM, D_MODEL, resident=True),
            _layer_spec(layer, 1, D_MODEL),
        ],
        out_specs=pl.BlockSpec((tm, D_MODEL), lambda i: (i, 0)),
        out_shape=jax.ShapeDtypeStruct((m, D_MODEL), F32),
        scratch_shapes=[
            pltpu.VMEM((tm, D_MODEL), BF16),
            pltpu.VMEM((tm, FFN_DIM), BF16),
            pltpu.VMEM((TAIL_ROWS, 2 * FFN_DIM), F32),
        ] + [pltpu.VMEM((1, tm, MXU_COLS), F32)] * (STAGE_DEPTH + 1),
        compiler_params=_params("arbitrary"),
        name="ffn",
    )(x, g_pre, w_up, conv_w, conv_b, w_down, g_post)


def _rows(v):
    return v.reshape(v.shape[0], 1, v.shape[1])


def kernel(x, mem, norm_pre_mix, norm_post_mix, norm_pre_mem, norm_mem_kv, norm_post_mem,
           norm_pre_ffn, norm_post_ffn, w_in, gm_v_norm, gm_w_s, gm_b_s, ssd_conv_w, ssd_conv_b,
           ssd_dt_bias, ssd_a_log, ssd_d, ssd_norm, w_branch_a, w_branch_b, w_out,
           xa_w_q, xa_w_kv, xa_w_o, ffn_w_up, ffn_conv_w, ffn_conv_b, ffn_w_down):
    batch, seq, _ = x.shape
    depth = w_in.shape[0]
    xf = x.reshape(batch * seq, D_MODEL)
    memf = mem.reshape(batch * MEM_LEN, D_MODEL)

    w_front = w_in.astype(BF16)
    w_gates = w_front[:, :, DT_OFF + SSD_HEADS:]
    w_dt = jnp.pad(w_front[:, :, DT_OFF:DT_OFF + SSD_HEADS], ((0, 0), (0, 0), (0, LANES - SSD_HEADS)))
    pad_heads = lambda v: _rows(jnp.pad(v, ((0, 0), (0, LANES - SSD_HEADS))))
    b_exp = jnp.repeat(jnp.swapaxes(gm_b_s, 1, 2), LANES, axis=2)
    d_exp = _rows(jnp.repeat(ssd_d, SSD_HEAD_DIM, axis=1))
    e_mat = (lax.broadcasted_iota(jnp.int32, (2 * LANES, SSD_INNER), 0) % LANES
             == lax.broadcasted_iota(jnp.int32, (2 * LANES, SSD_INNER), 1) // SSD_HEAD_DIM).astype(BF16)
    bf = lambda w: w.astype(BF16)
    w_a, w_b, w_o = bf(w_branch_a), bf(w_branch_b), bf(w_out)
    w_q, w_kv, w_xo = bf(xa_w_q), bf(xa_w_kv), bf(xa_w_o)
    w_up, w_down = bf(ffn_w_up), bf(ffn_w_down)

    for l in range(depth):
        proj, dt = _inproj(l, xf, _rows(norm_pre_mix), w_front, w_gates, w_dt, _rows(gm_v_norm),
                           ssd_conv_w, _rows(ssd_conv_b), seq)
        ya, yb = _mixer(l, proj, dt, gm_w_s, b_exp, pad_heads(ssd_dt_bias), pad_heads(ssd_a_log),
                        d_exp, _rows(ssd_norm), e_mat, batch, seq)
        xf = _merge(l, ya, yb, proj, xf, w_a, w_b, w_o, _rows(norm_post_mix))

        kv = _kvproj(l, memf, _rows(norm_mem_kv), w_kv)
        xf = _xattn(l, xf, kv, _rows(norm_pre_mem), w_q, w_xo, _rows(norm_post_mem), seq)

        xf = _ffn(l, xf, _rows(norm_pre_ffn), w_up, ffn_conv_w, _rows(ffn_conv_b), w_down,
                  _rows(norm_post_ffn), seq)
    return xf.reshape(batch, seq, D_MODEL)
```

```python
import functools

import jax
import jax.numpy as jnp
from jax import lax
from jax.experimental import pallas as pl
from jax.experimental.pallas import tpu as pltpu

F32 = jnp.float32
BF16 = jnp.bfloat16

D_MODEL = 1024
MEM_LEN = 256
EPS = 1e-6
GM_GROUPS = 8
GM_WIDTH = 1024
CHUNK = 128
SSD_INNER = 2048
SSD_HEAD_DIM = 64
SSD_HEADS = 32
SSD_GROUPS = 4
SSD_HPG = 8
SSD_STATE = 128
SSD_CONV = 4
SSD_XBC = 3072
SSD_BC = 2 * SSD_GROUPS * SSD_STATE
GROUP_COLS = SSD_INNER // SSD_GROUPS
X_HEADS = 4
X_HEAD_DIM = 256
FFN_DIM = 2816
FFN_CONV = 3
LANES = 128
MXU_COLS = 256
TAIL_ROWS = 8
MAIN_COLS = 9216
DT_OFF = 7168
SEG_OFF = (0, 1024, 2048, 4096, 7168, 9216)
STAGE_DEPTH = 2
FFN_STAGE_DEPTH = 4
MIX_CHUNKS = 8
ROW_SUB = 512
FFN_DOWN_GROUP = 4

VMEM_LIMIT = 56 * 1024 * 1024


def _params(*sem):
    return pltpu.CompilerParams(dimension_semantics=sem, vmem_limit_bytes=VMEM_LIMIT)


def _layer_spec(layer, *shape, resident=False):
    kw = dict(pipeline_mode=pl.Buffered(1)) if resident else {}
    return pl.BlockSpec((None,) + shape, lambda *_: (layer,) + (0,) * len(shape), **kw)


def _dot(a, b):
    return jnp.dot(a, b, preferred_element_type=F32)


def _dot_nt(a, b):
    return lax.dot_general(a, b, (((1,), (1,)), ((), ())), preferred_element_type=F32)


def _rms(x, g):
    ms = jnp.mean(x * x, axis=-1, keepdims=True)
    return x * lax.rsqrt(ms + EPS) * g


GELU_C1 = 0.7978845608028654
GELU_C2 = GELU_C1 * 0.044715


def _gelu(x):
    half = 0.5 * x
    return half * jnp.tanh(x * (GELU_C1 + GELU_C2 * (x * x))) + half


def _sigmoid(x):
    return 0.5 * jnp.tanh(0.5 * x) + 0.5


def _silu(x):
    half = 0.5 * x
    return half * jnp.tanh(half) + half


def _softplus(x):
    return jnp.maximum(x, 0.0) + jnp.log1p(jnp.exp(-jnp.abs(x)))


def _causal_conv(cur, tail, w, b):
    k_width = w.shape[0]
    row = lax.broadcasted_iota(jnp.int32, tail.shape, 0)
    y = b + w[k_width - 1:k_width, :] * cur
    for s in range(1, k_width):
        r = pltpu.roll(cur, s, axis=0)
        rt = pltpu.roll(tail, s, axis=0)
        head = jnp.where(row < s, rt, r[0:TAIL_ROWS, :])
        r = jnp.concatenate([head, r[TAIL_ROWS:, :]], axis=0)
        y = y + w[k_width - 1 - s:k_width - s, :] * r
    return y


def _interleave(a, b):
    out = []
    for k in range(max(len(a), len(b))):
        out.extend(a[k:k + 1])
        out.extend(b[k:k + 1])
    return out


def _staged(n, produce, consume, bufs, slot0):
    depth = len(bufs) - 1
    for k in range(n + depth):
        if k < n:
            bufs[k % len(bufs)][slot0] = produce(k)
        if k >= depth:
            consume(k - depth, bufs[(k - depth) % len(bufs)][slot0])


def _inproj_kernel(tiles_per_seq, x_ref, g_ref, w_ref, wg_ref, wdt_ref, vgain_ref, cw_ref, cb_ref,
                   o_ref, dt_ref, h_scr, v_scr, tail_scr, *stage):
    i = pl.program_id(0)
    tm = x_ref.shape[0]
    sub = MXU_COLS
    h = _rms(x_ref[...], g_ref[...]).astype(BF16)
    h_scr[...] = h
    dt_ref[...] = _dot(h, wdt_ref[...])

    @pl.when(i % tiles_per_seq == 0)
    def _():
        tail_scr[...] = jnp.zeros_like(tail_scr)

    ssq = []

    def ep_u(raw, c0):
        o_ref[:, c0:c0 + sub] = _gelu(raw).astype(BF16)

    def ep_v(raw, c0):
        lc = c0 - SEG_OFF[1]
        gv = _gelu(raw)
        v_scr[:, lc:lc + sub] = gv
        ssq.append(jnp.sum(gv * gv, axis=-1, keepdims=True))
        if c0 + sub == SEG_OFF[2]:
            scale = lax.rsqrt(sum(ssq) * (1.0 / GM_WIDTH) + EPS)
            for d0 in range(0, GM_WIDTH, sub):
                o_ref[:, SEG_OFF[1] + d0:SEG_OFF[1] + d0 + sub] = (
                    v_scr[:, d0:d0 + sub] * scale * vgain_ref[:, d0:d0 + sub]).astype(BF16)

    def ep_z(raw, c0):
        o_ref[:, c0:c0 + sub] = _silu(raw).astype(BF16)

    def ep_xbc(raw, c0):
        lc = c0 - SEG_OFF[3]
        y = _causal_conv(raw, tail_scr[:, lc:lc + sub], cw_ref[:, lc:lc + sub], cb_ref[:, lc:lc + sub])
        tail_scr[:, lc:lc + sub] = raw[tm - TAIL_ROWS:, :]
        o_ref[:, c0:c0 + sub] = _silu(y).astype(BF16)

    def ep_gate(raw, c0):
        o_ref[:, c0:c0 + sub] = _sigmoid(raw).astype(BF16)

    def seg_tasks(*pairs):
        return [(fn, c0) for seg, fn in pairs for c0 in range(SEG_OFF[seg], SEG_OFF[seg + 1], sub)]

    tasks = seg_tasks((1, ep_v)) + _interleave(seg_tasks((3, ep_xbc), (0, ep_u)),
                                               seg_tasks((2, ep_z), (4, ep_gate)))

    def produce(k):
        c0 = tasks[k][1]
        if c0 < SEG_OFF[4]:
            return _dot(h_scr[...], w_ref[:, c0:c0 + sub])
        return _dot(h_scr[...], wg_ref[:, c0 - SEG_OFF[4]:c0 - SEG_OFF[4] + sub])

    _staged(len(tasks), produce, lambda k, raw: tasks[k][0](raw, tasks[k][1]),
            stage, jnp.minimum(i, 0))


def _inproj(layer, x, g, w_front, w_gates, w_dt, v_gain, conv_w, conv_b, seq, tm=512):
    m = x.shape[0]
    return pl.pallas_call(
        functools.partial(_inproj_kernel, seq // tm),
        grid=(m // tm,),
        in_specs=[
            pl.BlockSpec((tm, D_MODEL), lambda i: (i, 0)),
            _layer_spec(layer, 1, D_MODEL),
            _layer_spec(layer, D_MODEL, SEG_OFF[4], resident=True),
            _layer_spec(layer, D_MODEL, MAIN_COLS - SEG_OFF[4], resident=True),
            _layer_spec(layer, D_MODEL, LANES, resident=True),
            _layer_spec(layer, 1, GM_WIDTH),
            _layer_spec(layer, SSD_CONV, SSD_XBC),
            _layer_spec(layer, 1, SSD_XBC),
        ],
        out_specs=[
            pl.BlockSpec((tm, MAIN_COLS), lambda i: (i, 0)),
            pl.BlockSpec((tm, LANES), lambda i: (i, 0)),
        ],
        out_shape=[
            jax.ShapeDtypeStruct((m, MAIN_COLS), BF16),
            jax.ShapeDtypeStruct((m, LANES), F32),
        ],
        scratch_shapes=[
            pltpu.VMEM((tm, D_MODEL), BF16),
            pltpu.VMEM((tm, GM_WIDTH), F32),
            pltpu.VMEM((TAIL_ROWS, SSD_XBC), F32),
        ] + [pltpu.VMEM((1, tm, MXU_COLS), F32)] * (STAGE_DEPTH + 1),
        compiler_params=_params("arbitrary"),
        name="inproj",
    )(x, g, w_front, w_gates, w_dt, v_gain, conv_w, conv_b)


def _kvproj_kernel(x_ref, g_ref, w_ref, o_ref):
    h = _rms(x_ref[...], g_ref[...]).astype(BF16)
    o_ref[...] = _dot(h, w_ref[...]).astype(BF16)


def _kvproj(layer, mem, g, w_kv, tm=512):
    m = mem.shape[0]
    n = 2 * D_MODEL
    return pl.pallas_call(
        _kvproj_kernel,
        grid=(m // tm,),
        in_specs=[
            pl.BlockSpec((tm, D_MODEL), lambda i: (i, 0)),
            _layer_spec(layer, 1, D_MODEL),
            _layer_spec(layer, D_MODEL, n),
        ],
        out_specs=pl.BlockSpec((tm, n), lambda i: (i, 0)),
        out_shape=jax.ShapeDtypeStruct((m, n), BF16),
        compiler_params=_params("arbitrary"),
        name="kvproj",
    )(mem, g, w_kv)


def _split2(x):
    hi = x.astype(BF16)
    lo = (x - hi.astype(F32)).astype(BF16)
    return hi, lo


def _split3(x):
    hi = x.astype(BF16)
    r = x - hi.astype(F32)
    mid = r.astype(BF16)
    lo = (r - mid.astype(F32)).astype(BF16)
    return hi, mid, lo


def _mixer_kernel(u_ref, v_ref, z_ref, xs_ref, bc_ref, dt_ref,
                  ws_ref, bexp_ref, dtb_ref, alog_ref, dexp_ref, ng_ref, e_ref,
                  ya_ref, yb_ref,
                  state_scr, wsm_scr):
    c = pl.program_id(1)
    L = CHUNK
    rowi = lax.broadcasted_iota(jnp.int32, (L, L), 0)
    coli = lax.broadcasted_iota(jnp.int32, (L, L), 1)
    causal = rowi >= coli
    lane = lax.broadcasted_iota(jnp.int32, (L, LANES), 1)
    head_lane = lane < SSD_HEADS
    first_half = lane < SSD_HEAD_DIM
    zero_bf = jnp.zeros((L, LANES), BF16)
    tril = jnp.where(causal, 1.0, 0.0).astype(BF16)
    a_row = -jnp.exp(alog_ref[...])

    @pl.when(c == 0)
    def _():
        state_scr[...] = jnp.zeros_like(state_scr)
        for g in range(GM_GROUPS):
            wsm_scr[g] = jnp.where(causal, ws_ref[g], 0.0).astype(BF16)

    for ck in range(u_ref.shape[0] // L):
        rows = pl.ds(ck * L, L)
        for g in range(GM_GROUPS):
            sl = slice(g * LANES, (g + 1) * LANES)
            mixed = _dot(wsm_scr[g], v_ref[rows, sl]) + bexp_ref[:, sl]
            ya_ref[rows, sl] = (u_ref[rows, sl].astype(F32) * mixed).astype(BF16)

        dt = jnp.where(head_lane, _softplus(dt_ref[rows, :] + dtb_ref[...]), 0.0)
        da = dt * a_row
        d_hi, d_mid, d_lo = _split3(da)
        a_cs = _dot(tril, d_hi) + _dot(tril, d_mid) + _dot(tril, d_lo)
        a_cs_t = a_cs.T
        dt_t = dt.T
        ea2 = jnp.concatenate(_split2(jnp.exp(a_cs)), axis=1)
        ws2 = jnp.concatenate(_split2(dt * jnp.exp(a_cs[L - 1:L, :] - a_cs)), axis=1)

        for g in range(SSD_GROUPS):
            gsl = slice(g * GROUP_COLS, (g + 1) * GROUP_COLS)
            bm_bf = bc_ref[rows, g * SSD_STATE:(g + 1) * SSD_STATE]
            cm_bf = bc_ref[rows, SSD_GROUPS * SSD_STATE + g * SSD_STATE:
                           SSD_GROUPS * SSD_STATE + (g + 1) * SSD_STATE]
            cb = _dot_nt(cm_bf, bm_bf)
            xs_g_bf = xs_ref[rows, gsl]
            xs_g = xs_g_bf.astype(F32)
            e_g = e_ref[:, gsl]
            ea_exp = _dot(ea2, e_g)
            wst_exp = _dot(ws2, e_g)
            state_g = state_scr[:, gsl]
            y_g = _dot(cm_bf, state_g.astype(BF16)) * ea_exp
            parts = []
            for pr in range(SSD_HPG // 2):
                sc = []
                for hh in range(2):
                    h = g * SSD_HPG + 2 * pr + hh
                    seg = a_cs[:, h:h + 1] - a_cs_t[h:h + 1, :]
                    lm = jnp.exp(jnp.where(causal, seg, -jnp.inf))
                    sc.append((cb * lm * dt_t[h:h + 1, :]).astype(BF16))
                lhs = jnp.concatenate(sc, axis=1)
                x_pair = xs_g_bf[:, pr * LANES:(pr + 1) * LANES]
                rhs = jnp.concatenate([jnp.where(first_half, x_pair, zero_bf),
                                       jnp.where(first_half, zero_bf, x_pair)], axis=0)
                parts.append(_dot(lhs, rhs))
            y_g = y_g + jnp.concatenate(parts, axis=1) + xs_g * dexp_ref[:, gsl]
            xw = (xs_g * wst_exp).astype(BF16)
            new_state = state_g * ea_exp[L - 1:L, :] + _dot(bm_bf.astype(F32).T.astype(BF16), xw)
            state_scr[:, gsl] = new_state
            y_g = y_g * z_ref[rows, gsl].astype(F32)
            yb_ref[rows, gsl] = _rms(y_g, ng_ref[:, gsl]).astype(BF16)


def _mixer(layer, proj, dt, w_s, b_exp, dt_bias, a_log, d_exp, norm_g, e_mat, batch, seq):
    rows = MIX_CHUNKS * CHUNK
    steps = seq // rows
    row = lambda b, c: b * steps + c
    in_specs = [
        pl.BlockSpec((rows, GM_WIDTH), lambda b, c: (row(b, c), 0)),
        pl.BlockSpec((rows, GM_WIDTH), lambda b, c: (row(b, c), 1)),
        pl.BlockSpec((rows, SSD_INNER), lambda b, c: (row(b, c), 1)),
        pl.BlockSpec((rows, SSD_INNER), lambda b, c: (row(b, c), 2)),
        pl.BlockSpec((rows, SSD_BC), lambda b, c: (row(b, c), 6)),
        pl.BlockSpec((rows, LANES), lambda b, c: (row(b, c), 0)),
        _layer_spec(layer, GM_GROUPS, CHUNK, CHUNK),
        _layer_spec(layer, CHUNK, GM_WIDTH),
        _layer_spec(layer, 1, LANES),
        _layer_spec(layer, 1, LANES),
        _layer_spec(layer, 1, SSD_INNER),
        _layer_spec(layer, 1, SSD_INNER),
        pl.BlockSpec((2 * LANES, SSD_INNER), lambda b, c: (0, 0)),
    ]
    m = batch * seq
    return pl.pallas_call(
        _mixer_kernel,
        grid=(batch, steps),
        in_specs=in_specs,
        out_specs=[
            pl.BlockSpec((rows, GM_WIDTH), lambda b, c: (row(b, c), 0)),
            pl.BlockSpec((rows, SSD_INNER), lambda b, c: (row(b, c), 0)),
        ],
        out_shape=[
            jax.ShapeDtypeStruct((m, GM_WIDTH), BF16),
            jax.ShapeDtypeStruct((m, SSD_INNER), BF16),
        ],
        scratch_shapes=[
            pltpu.VMEM((SSD_STATE, SSD_INNER), F32),
            pltpu.VMEM((GM_GROUPS, CHUNK, CHUNK), BF16),
        ],
        compiler_params=_params("arbitrary", "arbitrary"),
        name="mixer",
    )(proj, proj, proj, proj, proj, dt, w_s, b_exp, dt_bias, a_log, d_exp, norm_g, e_mat)


def _merge_kernel(ya_ref, yb_ref, ga_ref, gb_ref, x_ref, wa_ref, wb_ref, wo_ref, g_ref, o_ref):
    for r0 in range(0, x_ref.shape[0], ROW_SUB):
        rows = pl.ds(r0, ROW_SUB)
        ya = _dot(ya_ref[rows, :], wa_ref[...])
        yb = _dot(yb_ref[rows, :], wb_ref[...])
        mixed = ga_ref[rows, :].astype(F32) * ya + gb_ref[rows, :].astype(F32) * yb
        y = _dot(mixed.astype(BF16), wo_ref[...])
        o_ref[rows, :] = x_ref[rows, :] + _rms(y, g_ref[...])


def _merge(layer, ya, yb, proj, x, w_a, w_b, w_o, g, tm=1024):
    m = x.shape[0]
    return pl.pallas_call(
        _merge_kernel,
        grid=(m // tm,),
        in_specs=[
            pl.BlockSpec((tm, GM_WIDTH), lambda i: (i, 0)),
            pl.BlockSpec((tm, SSD_INNER), lambda i: (i, 0)),
            pl.BlockSpec((tm, D_MODEL), lambda i: (i, 7)),
            pl.BlockSpec((tm, D_MODEL), lambda i: (i, 8)),
            pl.BlockSpec((tm, D_MODEL), lambda i: (i, 0)),
            _layer_spec(layer, GM_WIDTH, D_MODEL, resident=True),
            _layer_spec(layer, SSD_INNER, D_MODEL, resident=True),
            _layer_spec(layer, D_MODEL, D_MODEL, resident=True),
            _layer_spec(layer, 1, D_MODEL),
        ],
        out_specs=pl.BlockSpec((tm, D_MODEL), lambda i: (i, 0)),
        out_shape=jax.ShapeDtypeStruct((m, D_MODEL), F32),
        compiler_params=_params("arbitrary"),
        name="merge",
    )(ya, yb, proj, proj, x, w_a, w_b, w_o, g)


def _xattn_kernel(x_ref, gpre_ref, wq_ref, k_ref, v_ref, wo_ref, gpost_ref, o_ref):
    for r0 in range(0, x_ref.shape[0], ROW_SUB):
        rows = pl.ds(r0, ROW_SUB)
        x = x_ref[rows, :]
        h = _rms(x, gpre_ref[...]).astype(BF16)
        q = (_dot(h, wq_ref[...]) * (X_HEAD_DIM ** -0.5)).astype(BF16)
        outs = []
        for hd in range(X_HEADS):
            sl = slice(hd * X_HEAD_DIM, (hd + 1) * X_HEAD_DIM)
            s = _dot_nt(q[:, sl], k_ref[:, sl])
            e = jnp.exp(s - jnp.max(s, axis=-1, keepdims=True))
            p = e / jnp.sum(e, axis=-1, keepdims=True)
            outs.append(_dot(p.astype(BF16), v_ref[:, sl]).astype(BF16))
        o = jnp.concatenate(outs, axis=1)
        y = _dot(o, wo_ref[...])
        o_ref[rows, :] = x + _rms(y, gpost_ref[...])


def _xattn(layer, x, kv, g_pre, w_q, w_o, g_post, seq, tm=2048):
    m = x.shape[0]
    tiles = seq // tm
    return pl.pallas_call(
        _xattn_kernel,
        grid=(m // tm,),
        in_specs=[
            pl.BlockSpec((tm, D_MODEL), lambda i: (i, 0)),
            _layer_spec(layer, 1, D_MODEL),
            _layer_spec(layer, D_MODEL, D_MODEL),
            pl.BlockSpec((MEM_LEN, D_MODEL), lambda i: (i // tiles, 0)),
            pl.BlockSpec((MEM_LEN, D_MODEL), lambda i: (i // tiles, 1)),
            _layer_spec(layer, D_MODEL, D_MODEL),
            _layer_spec(layer, 1, D_MODEL),
        ],
        out_specs=pl.BlockSpec((tm, D_MODEL), lambda i: (i, 0)),
        out_shape=jax.ShapeDtypeStruct((m, D_MODEL), F32),
        compiler_params=_params("arbitrary"),
        name="xattn",
    )(x, g_pre, w_q, kv, kv, w_o, g_post)


def _ffn_kernel(tiles_per_seq, x_ref, gpre_ref, wup_ref, cw_ref, cb_ref, wd_ref, gpost_ref, o_ref,
                h_scr, g_scr, tail_scr, *stage):
    i = pl.program_id(0)
    tm = x_ref.shape[0]
    sub = MXU_COLS
    h_scr[...] = _rms(x_ref[...], gpre_ref[...]).astype(BF16)

    @pl.when(i % tiles_per_seq == 0)
    def _():
        tail_scr[...] = jnp.zeros_like(tail_scr)

    def produce(k):
        c0 = (k // 2) * sub + (k % 2) * FFN_DIM
        return _dot(h_scr[...], wup_ref[:, c0:c0 + sub])

    pending = {}
    down = []
    n_sub = FFN_DIM // sub

    def consume(k, raw):
        c0 = (k // 2) * sub + (k % 2) * FFN_DIM
        y = _causal_conv(raw, tail_scr[:, c0:c0 + sub], cw_ref[:, c0:c0 + sub], cb_ref[:, c0:c0 + sub])
        tail_scr[:, c0:c0 + sub] = raw[tm - TAIL_ROWS:, :]
        if k % 2 == 0:
            pending[k // 2] = _gelu(y)
            return
        j = k // 2
        g_scr[:, j * sub:(j + 1) * sub] = (pending.pop(j) * y).astype(BF16)
        if (j + 1) % FFN_DOWN_GROUP == 0 or j == n_sub - 1:
            lo = (j // FFN_DOWN_GROUP) * FFN_DOWN_GROUP * sub
            down.append(_dot(g_scr[:, lo:(j + 1) * sub], wd_ref[lo:(j + 1) * sub, :]))

    _staged(2 * n_sub, produce, consume, stage, jnp.minimum(i, 0))

    o_ref[...] = x_ref[...] + _rms(sum(down), gpost_ref[...])


def _ffn(layer, x, g_pre, w_up, conv_w, conv_b, w_down, g_post, seq, tm=512):
    m = x.shape[0]
    return pl.pallas_call(
        functools.partial(_ffn_kernel, seq // tm),
        grid=(m // tm,),
        in_specs=[
            pl.BlockSpec((tm, D_MODEL), lambda i: (i, 0)),
            _layer_spec(layer, 1, D_MODEL),
            _layer_spec(layer, D_MODEL, 2 * FFN_DIM, resident=True),
            _layer_spec(layer, FFN_CONV, 2 * FFN_DIM),
            _layer_spec(layer, 1, 2 * FFN_DIM),
            _layer_spec(layer, FFN_DIM, D_MODEL, resident=True),
            _layer_spec(layer, 1, D_MODEL),
        ],
        out_specs=pl.BlockSpec((tm, D_MODEL), lambda i: (i, 0)),
        out_shape=jax.ShapeDtypeStruct((m, D_MODEL), F32),
        scratch_shapes=[
            pltpu.VMEM((tm, D_MODEL), BF16),
            pltpu.VMEM((tm, FFN_DIM), BF16),
            pltpu.VMEM((TAIL_ROWS, 2 * FFN_DIM), F32),
        ] + [pltpu.VMEM((1, tm, MXU_COLS), F32)] * (FFN_STAGE_DEPTH + 1),
        compiler_params=_params("arbitrary"),
        name="ffn",
    )(x, g_pre, w_up, conv_w, conv_b, w_down, g_post)


def _cast_kernel(x_ref, o_ref):
    o_ref[...] = x_ref[...].astype(BF16)


def _cast_bf16(w, rows=256):
    depth, r, c = w.shape
    spec = pl.BlockSpec((None, rows, c), lambda l, i: (l, i, 0))
    return pl.pallas_call(
        _cast_kernel,
        grid=(depth, r // rows),
        in_specs=[spec],
        out_specs=spec,
        out_shape=jax.ShapeDtypeStruct(w.shape, BF16),
        compiler_params=_params("arbitrary", "arbitrary"),
        name="cast",
    )(w)


def _rows(v):
    return v.reshape(v.shape[0], 1, v.shape[1])


def kernel(x, mem, norm_pre_mix, norm_post_mix, norm_pre_mem, norm_mem_kv, norm_post_mem,
           norm_pre_ffn, norm_post_ffn, w_in, gm_v_norm, gm_w_s, gm_b_s, ssd_conv_w, ssd_conv_b,
           ssd_dt_bias, ssd_a_log, ssd_d, ssd_norm, w_branch_a, w_branch_b, w_out,
           xa_w_q, xa_w_kv, xa_w_o, ffn_w_up, ffn_conv_w, ffn_conv_b, ffn_w_down):
    batch, seq, _ = x.shape
    depth = w_in.shape[0]
    xf = x.reshape(batch * seq, D_MODEL)
    memf = mem.reshape(batch * MEM_LEN, D_MODEL)

    w_front = _cast_bf16(w_in)
    w_gates = w_front[:, :, DT_OFF + SSD_HEADS:]
    w_dt = jnp.pad(w_front[:, :, DT_OFF:DT_OFF + SSD_HEADS], ((0, 0), (0, 0), (0, LANES - SSD_HEADS)))
    pad_heads = lambda v: _rows(jnp.pad(v, ((0, 0), (0, LANES - SSD_HEADS))))
    b_exp = jnp.repeat(jnp.swapaxes(gm_b_s, 1, 2), LANES, axis=2)
    d_exp = _rows(jnp.repeat(ssd_d, SSD_HEAD_DIM, axis=1))
    e_mat = (lax.broadcasted_iota(jnp.int32, (2 * LANES, SSD_INNER), 0) % LANES
             == lax.broadcasted_iota(jnp.int32, (2 * LANES, SSD_INNER), 1) // SSD_HEAD_DIM).astype(BF16)
    bf = lambda w: w.astype(BF16)
    w_a, w_b, w_o = bf(w_branch_a), bf(w_branch_b), bf(w_out)
    w_q, w_kv, w_xo = bf(xa_w_q), bf(xa_w_kv), bf(xa_w_o)
    w_up, w_down = bf(ffn_w_up), bf(ffn_w_down)

    for l in range(depth):
        proj, dt = _inproj(l, xf, _rows(norm_pre_mix), w_front, w_gates, w_dt, _rows(gm_v_norm),
                           ssd_conv_w, _rows(ssd_conv_b), seq)
        ya, yb = _mixer(l, proj, dt, gm_w_s, b_exp, pad_heads(ssd_dt_bias), pad_heads(ssd_a_log),
                        d_exp, _rows(ssd_norm), e_mat, batch, seq)
        xf = _merge(l, ya, yb, proj, xf, w_a, w_b, w_o, _rows(norm_post_mix))

        kv = _kvproj(l, memf, _rows(norm_mem_kv), w_kv)
        xf = _xattn(l, xf, kv, _rows(norm_pre_mem), w_q, w_xo, _rows(norm_post_mem), seq)

        xf = _ffn(l, xf, _rows(norm_pre_ffn), w_up, ffn_conv_w, _rows(ffn_conv_b), w_down,
                  _rows(norm_post_ffn), seq)
    return xf.reshape(batch, seq, D_MODEL)
```

```python
import functools

import jax
import jax.numpy as jnp
from jax import lax
from jax.experimental import pallas as pl
from jax.experimental.pallas import tpu as pltpu

F32 = jnp.float32
BF16 = jnp.bfloat16

D_MODEL = 1024
MEM_LEN = 256
EPS = 1e-6
GM_GROUPS = 8
GM_WIDTH = 1024
CHUNK = 128
SSD_INNER = 2048
SSD_HEAD_DIM = 64
SSD_HEADS = 32
SSD_GROUPS = 4
SSD_HPG = 8
SSD_STATE = 128
SSD_CONV = 4
SSD_XBC = 3072
SSD_BC = 2 * SSD_GROUPS * SSD_STATE
GROUP_COLS = SSD_INNER // SSD_GROUPS
X_HEADS = 4
X_HEAD_DIM = 256
FFN_DIM = 2816
FFN_CONV = 3
LANES = 128
MXU_COLS = 256
TAIL_ROWS = 8
MAIN_COLS = 9216
DT_OFF = 7168
SEG_OFF = (0, 1024, 2048, 4096, 7168, 9216)
STAGE_DEPTH = 2
FFN_STAGE_DEPTH = 4
MIX_CHUNKS = 4
ROW_SUB = 512
FFN_DOWN_GROUP = 4

VMEM_LIMIT = 56 * 1024 * 1024


def _params(*sem):
    return pltpu.CompilerParams(dimension_semantics=sem, vmem_limit_bytes=VMEM_LIMIT)


def _layer_spec(layer, *shape, resident=False):
    kw = dict(pipeline_mode=pl.Buffered(1)) if resident else {}
    return pl.BlockSpec((None,) + shape, lambda *_: (layer,) + (0,) * len(shape), **kw)


def _dot(a, b):
    return jnp.dot(a, b, preferred_element_type=F32)


def _dot_nt(a, b):
    return lax.dot_general(a, b, (((1,), (1,)), ((), ())), preferred_element_type=F32)


def _rms(x, g):
    ms = jnp.mean(x * x, axis=-1, keepdims=True)
    return x * lax.rsqrt(ms + EPS) * g


GELU_C1 = 0.7978845608028654
GELU_C2 = GELU_C1 * 0.044715


def _gelu(x):
    half = 0.5 * x
    return half * jnp.tanh(x * (GELU_C1 + GELU_C2 * (x * x))) + half


def _sigmoid(x):
    return 0.5 * jnp.tanh(0.5 * x) + 0.5


def _silu(x):
    half = 0.5 * x
    return half * jnp.tanh(half) + half


def _softplus(x):
    return jnp.maximum(x, 0.0) + jnp.log1p(jnp.exp(-jnp.abs(x)))


def _causal_conv(cur, tail, w, b):
    k_width = w.shape[0]
    row = lax.broadcasted_iota(jnp.int32, tail.shape, 0)
    y = b + w[k_width - 1:k_width, :] * cur
    for s in range(1, k_width):
        r = pltpu.roll(cur, s, axis=0)
        rt = pltpu.roll(tail, s, axis=0)
        head = jnp.where(row < s, rt, r[0:TAIL_ROWS, :])
        r = jnp.concatenate([head, r[TAIL_ROWS:, :]], axis=0)
        y = y + w[k_width - 1 - s:k_width - s, :] * r
    return y


def _interleave(a, b):
    out = []
    for k in range(max(len(a), len(b))):
        out.extend(a[k:k + 1])
        out.extend(b[k:k + 1])
    return out


def _staged(n, produce, consume, bufs, slot0):
    depth = len(bufs) - 1
    for k in range(n + depth):
        if k < n:
            bufs[k % len(bufs)][slot0] = produce(k)
        if k >= depth:
            consume(k - depth, bufs[(k - depth) % len(bufs)][slot0])


def _inproj_kernel(tiles_per_seq, x_ref, g_ref, w_ref, wg_ref, wdt_ref, vgain_ref, cw_ref, cb_ref,
                   o_ref, dt_ref, h_scr, v_scr, tail_scr, *stage):
    i = pl.program_id(0)
    tm = x_ref.shape[0]
    sub = MXU_COLS
    h = _rms(x_ref[...], g_ref[...]).astype(BF16)
    h_scr[...] = h
    dt_ref[...] = _dot(h, wdt_ref[...])

    @pl.when(i % tiles_per_seq == 0)
    def _():
        tail_scr[...] = jnp.zeros_like(tail_scr)

    ssq = []

    def ep_u(raw, c0):
        o_ref[:, c0:c0 + sub] = _gelu(raw).astype(BF16)

    def ep_v(raw, c0):
        lc = c0 - SEG_OFF[1]
        gv = _gelu(raw)
        v_scr[:, lc:lc + sub] = gv
        ssq.append(jnp.sum(gv * gv, axis=-1, keepdims=True))
        if c0 + sub == SEG_OFF[2]:
            scale = lax.rsqrt(sum(ssq) * (1.0 / GM_WIDTH) + EPS)
            for d0 in range(0, GM_WIDTH, sub):
                o_ref[:, SEG_OFF[1] + d0:SEG_OFF[1] + d0 + sub] = (
                    v_scr[:, d0:d0 + sub] * scale * vgain_ref[:, d0:d0 + sub]).astype(BF16)

    def ep_z(raw, c0):
        o_ref[:, c0:c0 + sub] = _silu(raw).astype(BF16)

    def ep_xbc(raw, c0):
        lc = c0 - SEG_OFF[3]
        y = _causal_conv(raw, tail_scr[:, lc:lc + sub], cw_ref[:, lc:lc + sub], cb_ref[:, lc:lc + sub])
        tail_scr[:, lc:lc + sub] = raw[tm - TAIL_ROWS:, :]
        o_ref[:, c0:c0 + sub] = _silu(y).astype(BF16)

    def ep_gate(raw, c0):
        o_ref[:, c0:c0 + sub] = _sigmoid(raw).astype(BF16)

    def seg_tasks(*pairs):
        return [(fn, c0) for seg, fn in pairs for c0 in range(SEG_OFF[seg], SEG_OFF[seg + 1], sub)]

    tasks = seg_tasks((1, ep_v)) + _interleave(seg_tasks((3, ep_xbc), (0, ep_u)),
                                               seg_tasks((2, ep_z), (4, ep_gate)))

    def produce(k):
        c0 = tasks[k][1]
        if c0 < SEG_OFF[4]:
            return _dot(h_scr[...], w_ref[:, c0:c0 + sub])
        return _dot(h_scr[...], wg_ref[:, c0 - SEG_OFF[4]:c0 - SEG_OFF[4] + sub])

    _staged(len(tasks), produce, lambda k, raw: tasks[k][0](raw, tasks[k][1]),
            stage, jnp.minimum(i, 0))


def _inproj(layer, x, g, w_front, w_gates, w_dt, v_gain, conv_w, conv_b, seq, tm=512):
    m = x.shape[0]
    return pl.pallas_call(
        functools.partial(_inproj_kernel, seq // tm),
        grid=(m // tm,),
        in_specs=[
            pl.BlockSpec((tm, D_MODEL), lambda i: (i, 0)),
            _layer_spec(layer, 1, D_MODEL),
            _layer_spec(layer, D_MODEL, SEG_OFF[4], resident=True),
            _layer_spec(layer, D_MODEL, MAIN_COLS - SEG_OFF[4], resident=True),
            _layer_spec(layer, D_MODEL, LANES, resident=True),
            _layer_spec(layer, 1, GM_WIDTH),
            _layer_spec(layer, SSD_CONV, SSD_XBC),
            _layer_spec(layer, 1, SSD_XBC),
        ],
        out_specs=[
            pl.BlockSpec((tm, MAIN_COLS), lambda i: (i, 0)),
            pl.BlockSpec((tm, LANES), lambda i: (i, 0)),
        ],
        out_shape=[
            jax.ShapeDtypeStruct((m, MAIN_COLS), BF16),
            jax.ShapeDtypeStruct((m, LANES), F32),
        ],
        scratch_shapes=[
            pltpu.VMEM((tm, D_MODEL), BF16),
            pltpu.VMEM((tm, GM_WIDTH), F32),
            pltpu.VMEM((TAIL_ROWS, SSD_XBC), F32),
        ] + [pltpu.VMEM((1, tm, MXU_COLS), F32)] * (STAGE_DEPTH + 1),
        compiler_params=_params("arbitrary"),
        name="inproj",
    )(x, g, w_front, w_gates, w_dt, v_gain, conv_w, conv_b)


def _kvproj_kernel(x_ref, g_ref, w_ref, o_ref):
    h = _rms(x_ref[...], g_ref[...]).astype(BF16)
    o_ref[...] = _dot(h, w_ref[...]).astype(BF16)


def _kvproj(layer, mem, g, w_kv, tm=512):
    m = mem.shape[0]
    n = 2 * D_MODEL
    return pl.pallas_call(
        _kvproj_kernel,
        grid=(m // tm,),
        in_specs=[
            pl.BlockSpec((tm, D_MODEL), lambda i: (i, 0)),
            _layer_spec(layer, 1, D_MODEL),
            _layer_spec(layer, D_MODEL, n),
        ],
        out_specs=pl.BlockSpec((tm, n), lambda i: (i, 0)),
        out_shape=jax.ShapeDtypeStruct((m, n), BF16),
        compiler_params=_params("arbitrary"),
        name="kvproj",
    )(mem, g, w_kv)


def _split2(x):
    hi = x.astype(BF16)
    lo = (x - hi.astype(F32)).astype(BF16)
    return hi, lo


def _split3(x):
    hi = x.astype(BF16)
    r = x - hi.astype(F32)
    mid = r.astype(BF16)
    lo = (r - mid.astype(F32)).astype(BF16)
    return hi, mid, lo


def _mixer_kernel(u_ref, v_ref, z_ref, xs_ref, bc_ref, dt_ref,
                  ws_ref, bexp_ref, dtb_ref, alog_ref, dexp_ref, ng_ref, e_ref,
                  ya_ref, yb_ref,
                  state_scr, wsm_scr):
    c = pl.program_id(1)
    L = CHUNK
    rowi = lax.broadcasted_iota(jnp.int32, (L, L), 0)
    coli = lax.broadcasted_iota(jnp.int32, (L, L), 1)
    causal = rowi >= coli
    lane = lax.broadcasted_iota(jnp.int32, (L, LANES), 1)
    head_lane = lane < SSD_HEADS
    first_half = lane < SSD_HEAD_DIM
    zero_bf = jnp.zeros((L, LANES), BF16)
    tril = jnp.where(causal, 1.0, 0.0).astype(BF16)
    a_row = -jnp.exp(alog_ref[...])

    @pl.when(c == 0)
    def _():
        state_scr[...] = jnp.zeros_like(state_scr)
        for g in range(GM_GROUPS):
            wsm_scr[g] = jnp.where(causal, ws_ref[g], 0.0).astype(BF16)

    for ck in range(u_ref.shape[0] // L):
        rows = pl.ds(ck * L, L)
        for g in range(GM_GROUPS):
            sl = slice(g * LANES, (g + 1) * LANES)
            mixed = _dot(wsm_scr[g], v_ref[rows, sl]) + bexp_ref[:, sl]
            ya_ref[rows, sl] = (u_ref[rows, sl].astype(F32) * mixed).astype(BF16)

        dt = jnp.where(head_lane, _softplus(dt_ref[rows, :] + dtb_ref[...]), 0.0)
        da = dt * a_row
        d_hi, d_mid, d_lo = _split3(da)
        a_cs = _dot(tril, d_hi) + _dot(tril, d_mid) + _dot(tril, d_lo)
        a_cs_t = a_cs.T
        dt_t = dt.T
        ea2 = jnp.concatenate(_split2(jnp.exp(a_cs)), axis=1)
        ws2 = jnp.concatenate(_split2(dt * jnp.exp(a_cs[L - 1:L, :] - a_cs)), axis=1)

        for g in range(SSD_GROUPS):
            gsl = slice(g * GROUP_COLS, (g + 1) * GROUP_COLS)
            bm_bf = bc_ref[rows, g * SSD_STATE:(g + 1) * SSD_STATE]
            cm_bf = bc_ref[rows, SSD_GROUPS * SSD_STATE + g * SSD_STATE:
                           SSD_GROUPS * SSD_STATE + (g + 1) * SSD_STATE]
            cb = _dot_nt(cm_bf, bm_bf)
            xs_g_bf = xs_ref[rows, gsl]
            xs_g = xs_g_bf.astype(F32)
            e_g = e_ref[:, gsl]
            ea_exp = _dot(ea2, e_g)
            wst_exp = _dot(ws2, e_g)
            state_g = state_scr[:, gsl]
            y_g = _dot(cm_bf, state_g.astype(BF16)) * ea_exp
            parts = []
            for pr in range(SSD_HPG // 2):
                sc = []
                for hh in range(2):
                    h = g * SSD_HPG + 2 * pr + hh
                    seg = a_cs[:, h:h + 1] - a_cs_t[h:h + 1, :]
                    lm = jnp.exp(jnp.where(causal, seg, -jnp.inf))
                    sc.append((cb * lm * dt_t[h:h + 1, :]).astype(BF16))
                lhs = jnp.concatenate(sc, axis=1)
                x_pair = xs_g_bf[:, pr * LANES:(pr + 1) * LANES]
                rhs = jnp.concatenate([jnp.where(first_half, x_pair, zero_bf),
                                       jnp.where(first_half, zero_bf, x_pair)], axis=0)
                parts.append(_dot(lhs, rhs))
            y_g = y_g + jnp.concatenate(parts, axis=1) + xs_g * dexp_ref[:, gsl]
            xw = (xs_g * wst_exp).astype(BF16)
            new_state = state_g * ea_exp[L - 1:L, :] + _dot(bm_bf.astype(F32).T.astype(BF16), xw)
            state_scr[:, gsl] = new_state
            y_g = y_g * z_ref[rows, gsl].astype(F32)
            yb_ref[rows, gsl] = _rms(y_g, ng_ref[:, gsl]).astype(BF16)


def _mixer(layer, proj, dt, w_s, b_exp, dt_bias, a_log, d_exp, norm_g, e_mat, batch, seq):
    rows = MIX_CHUNKS * CHUNK
    steps = seq // rows
    row = lambda b, c: b * steps + c
    in_specs = [
        pl.BlockSpec((rows, GM_WIDTH), lambda b, c: (row(b, c), 0)),
        pl.BlockSpec((rows, GM_WIDTH), lambda b, c: (row(b, c), 1)),
        pl.BlockSpec((rows, SSD_INNER), lambda b, c: (row(b, c), 1)),
        pl.BlockSpec((rows, SSD_INNER), lambda b, c: (row(b, c), 2)),
        pl.BlockSpec((rows, SSD_BC), lambda b, c: (row(b, c), 6)),
        pl.BlockSpec((rows, LANES), lambda b, c: (row(b, c), 0)),
        _layer_spec(layer, GM_GROUPS, CHUNK, CHUNK),
        _layer_spec(layer, CHUNK, GM_WIDTH),
        _layer_spec(layer, 1, LANES),
        _layer_spec(layer, 1, LANES),
        _layer_spec(layer, 1, SSD_INNER),
        _layer_spec(layer, 1, SSD_INNER),
        pl.BlockSpec((2 * LANES, SSD_INNER), lambda b, c: (0, 0)),
    ]
    m = batch * seq
    return pl.pallas_call(
        _mixer_kernel,
        grid=(batch, steps),
        in_specs=in_specs,
        out_specs=[
            pl.BlockSpec((rows, GM_WIDTH), lambda b, c: (row(b, c), 0)),
            pl.BlockSpec((rows, SSD_INNER), lambda b, c: (row(b, c), 0)),
        ],
        out_shape=[
            jax.ShapeDtypeStruct((m, GM_WIDTH), BF16),
            jax.ShapeDtypeStruct((m, SSD_INNER), BF16),
        ],
        scratch_shapes=[
            pltpu.VMEM((SSD_STATE, SSD_INNER), F32),
            pltpu.VMEM((GM_GROUPS, CHUNK, CHUNK), BF16),
        ],
        compiler_params=_params("arbitrary", "arbitrary"),
        name="mixer",
    )(proj, proj, proj, proj, proj, dt, w_s, b_exp, dt_bias, a_log, d_exp, norm_g, e_mat)


def _merge_kernel(ya_ref, yb_ref, ga_ref, gb_ref, x_ref, wa_ref, wb_ref, wo_ref, g_ref, o_ref):
    for r0 in range(0, x_ref.shape[0], ROW_SUB):
        rows = pl.ds(r0, ROW_SUB)
        ya = _dot(ya_ref[rows, :], wa_ref[...])
        yb = _dot(yb_ref[rows, :], wb_ref[...])
        mixed = ga_ref[rows, :].astype(F32) * ya + gb_ref[rows, :].astype(F32) * yb
        y = _dot(mixed.astype(BF16), wo_ref[...])
        o_ref[rows, :] = x_ref[rows, :] + _rms(y, g_ref[...])


def _merge(layer, ya, yb, proj, x, w_a, w_b, w_o, g, tm=1024):
    m = x.shape[0]
    return pl.pallas_call(
        _merge_kernel,
        grid=(m // tm,),
        in_specs=[
            pl.BlockSpec((tm, GM_WIDTH), lambda i: (i, 0)),
            pl.BlockSpec((tm, SSD_INNER), lambda i: (i, 0)),
            pl.BlockSpec((tm, D_MODEL), lambda i: (i, 7)),
            pl.BlockSpec((tm, D_MODEL), lambda i: (i, 8)),
            pl.BlockSpec((tm, D_MODEL), lambda i: (i, 0)),
            _layer_spec(layer, GM_WIDTH, D_MODEL, resident=True),
            _layer_spec(layer, SSD_INNER, D_MODEL, resident=True),
            _layer_spec(layer, D_MODEL, D_MODEL, resident=True),
            _layer_spec(layer, 1, D_MODEL),
        ],
        out_specs=pl.BlockSpec((tm, D_MODEL), lambda i: (i, 0)),
        out_shape=jax.ShapeDtypeStruct((m, D_MODEL), F32),
        compiler_params=_params("arbitrary"),
        name="merge",
    )(ya, yb, proj, proj, x, w_a, w_b, w_o, g)


def _xattn_kernel(x_ref, gpre_ref, wq_ref, k_ref, v_ref, wo_ref, gpost_ref, o_ref):
    for r0 in range(0, x_ref.shape[0], ROW_SUB):
        rows = pl.ds(r0, ROW_SUB)
        x = x_ref[rows, :]
        h = _rms(x, gpre_ref[...]).astype(BF16)
        q = (_dot(h, wq_ref[...]) * (X_HEAD_DIM ** -0.5)).astype(BF16)
        outs = []
        for hd in range(X_HEADS):
            sl = slice(hd * X_HEAD_DIM, (hd + 1) * X_HEAD_DIM)
            s = _dot_nt(q[:, sl], k_ref[:, sl])
            e = jnp.exp(s - jnp.max(s, axis=-1, keepdims=True))
            p = e / jnp.sum(e, axis=-1, keepdims=True)
            outs.append(_dot(p.astype(BF16), v_ref[:, sl]).astype(BF16))
        o = jnp.concatenate(outs, axis=1)
        y = _dot(o, wo_ref[...])
        o_ref[rows, :] = x + _rms(y, gpost_ref[...])


def _xattn(layer, x, kv, g_pre, w_q, w_o, g_post, seq, tm=2048):
    m = x.shape[0]
    tiles = seq // tm
    return pl.pallas_call(
        _xattn_kernel,
        grid=(m // tm,),
        in_specs=[
            pl.BlockSpec((tm, D_MODEL), lambda i: (i, 0)),
            _layer_spec(layer, 1, D_MODEL),
            _layer_spec(layer, D_MODEL, D_MODEL),
            pl.BlockSpec((MEM_LEN, D_MODEL), lambda i: (i // tiles, 0)),
            pl.BlockSpec((MEM_LEN, D_MODEL), lambda i: (i // tiles, 1)),
            _layer_spec(layer, D_MODEL, D_MODEL),
            _layer_spec(layer, 1, D_MODEL),
        ],
        out_specs=pl.BlockSpec((tm, D_MODEL), lambda i: (i, 0)),
        out_shape=jax.ShapeDtypeStruct((m, D_MODEL), F32),
        compiler_params=_params("arbitrary"),
        name="xattn",
    )(x, g_pre, w_q, kv, kv, w_o, g_post)


def _ffn_kernel(tiles_per_seq, x_ref, gpre_ref, wup_ref, cw_ref, cb_ref, wd_ref, gpost_ref, o_ref,
                h_scr, g_scr, tail_scr, *stage):
    i = pl.program_id(0)
    tm = x_ref.shape[0]
    sub = MXU_COLS
    h_scr[...] = _rms(x_ref[...], gpre_ref[...]).astype(BF16)

    @pl.when(i % tiles_per_seq == 0)
    def _():
        tail_scr[...] = jnp.zeros_like(tail_scr)

    def produce(k):
        c0 = (k // 2) * sub + (k % 2) * FFN_DIM
        return _dot(h_scr[...], wup_ref[:, c0:c0 + sub])

    pending = {}
    down = []
    n_sub = FFN_DIM // sub

    def consume(k, raw):
        c0 = (k // 2) * sub + (k % 2) * FFN_DIM
        y = _causal_conv(raw, tail_scr[:, c0:c0 + sub], cw_ref[:, c0:c0 + sub], cb_ref[:, c0:c0 + sub])
        tail_scr[:, c0:c0 + sub] = raw[tm - TAIL_ROWS:, :]
        if k % 2 == 0:
            pending[k // 2] = _gelu(y)
            return
        j = k // 2
        g_scr[:, j * sub:(j + 1) * sub] = (pending.pop(j) * y).astype(BF16)
        if (j + 1) % FFN_DOWN_GROUP == 0 or j == n_sub - 1:
            lo = (j // FFN_DOWN_GROUP) * FFN_DOWN_GROUP * sub
            down.append(_dot(g_scr[:, lo:(j + 1) * sub], wd_ref[lo:(j + 1) * sub, :]))

    _staged(2 * n_sub, produce, consume, stage, jnp.minimum(i, 0))

    o_ref[...] = x_ref[...] + _rms(sum(down), gpost_ref[...])


def _ffn(layer, x, g_pre, w_up, conv_w, conv_b, w_down, g_post, seq, tm=512):
    m = x.shape[0]
    return pl.pallas_call(
        functools.partial(_ffn_kernel, seq // tm),
        grid=(m // tm,),
        in_specs=[
            pl.BlockSpec((tm, D_MODEL), lambda i: (i, 0)),
            _layer_spec(layer, 1, D_MODEL),
            _layer_spec(layer, D_MODEL, 2 * FFN_DIM, resident=True),
            _layer_spec(layer, FFN_CONV, 2 * FFN_DIM),
            _layer_spec(layer, 1, 2 * FFN_DIM),
            _layer_spec(layer, FFN_DIM, D_MODEL, resident=True),
            _layer_spec(layer, 1, D_MODEL),
        ],
        out_specs=pl.BlockSpec((tm, D_MODEL), lambda i: (i, 0)),
        out_shape=jax.ShapeDtypeStruct((m, D_MODEL), F32),
        scratch_shapes=[
            pltpu.VMEM((tm, D_MODEL), BF16),
            pltpu.VMEM((tm, FFN_DIM), BF16),
            pltpu.VMEM((TAIL_ROWS, 2 * FFN_DIM), F32),
        ] + [pltpu.VMEM((1, tm, MXU_COLS), F32)] * (FFN_STAGE_DEPTH + 1),
        compiler_params=_params("arbitrary"),
        name="ffn",
    )(x, g_pre, w_up, conv_w, conv_b, w_down, g_post)


def _rows(v):
    return v.reshape(v.shape[0], 1, v.shape[1])


def kernel(x, mem, norm_pre_mix, norm_post_mix, norm_pre_mem, norm_mem_kv, norm_post_mem,
           norm_pre_ffn, norm_post_ffn, w_in, gm_v_norm, gm_w_s, gm_b_s, ssd_conv_w, ssd_conv_b,
           ssd_dt_bias, ssd_a_log, ssd_d, ssd_norm, w_branch_a, w_branch_b, w_out,
           xa_w_q, xa_w_kv, xa_w_o, ffn_w_up, ffn_conv_w, ffn_conv_b, ffn_w_down):
    batch, seq, _ = x.shape
    depth = w_in.shape[0]
    xf = x.reshape(batch * seq, D_MODEL)
    memf = mem.reshape(batch * MEM_LEN, D_MODEL)

    w_front = w_in.astype(BF16)
    w_gates = w_front[:, :, DT_OFF + SSD_HEADS:]
    w_dt = jnp.pad(w_front[:, :, DT_OFF:DT_OFF + SSD_HEADS], ((0, 0), (0, 0), (0, LANES - SSD_HEADS)))
    pad_heads = lambda v: _rows(jnp.pad(v, ((0, 0), (0, LANES - SSD_HEADS))))
    b_exp = jnp.repeat(jnp.swapaxes(gm_b_s, 1, 2), LANES, axis=2)
    d_exp = _rows(jnp.repeat(ssd_d, SSD_HEAD_DIM, axis=1))
    e_mat = (lax.broadcasted_iota(jnp.int32, (2 * LANES, SSD_INNER), 0) % LANES
             == lax.broadcasted_iota(jnp.int32, (2 * LANES, SSD_INNER), 1) // SSD_HEAD_DIM).astype(BF16)
    bf = lambda w: w.astype(BF16)
    w_a, w_b, w_o = bf(w_branch_a), bf(w_branch_b), bf(w_out)
    w_q, w_kv, w_xo = bf(xa_w_q), bf(xa_w_kv), bf(xa_w_o)
    w_up, w_down = bf(ffn_w_up), bf(ffn_w_down)

    for l in range(depth):
        proj, dt = _inproj(l, xf, _rows(norm_pre_mix), w_front, w_gates, w_dt, _rows(gm_v_norm),
                           ssd_conv_w, _rows(ssd_conv_b), seq)
        ya, yb = _mixer(l, proj, dt, gm_w_s, b_exp, pad_heads(ssd_dt_bias), pad_heads(ssd_a_log),
                        d_exp, _rows(ssd_norm), e_mat, batch, seq)
        xf = _merge(l, ya, yb, proj, xf, w_a, w_b, w_o, _rows(norm_post_mix))

        kv = _kvproj(l, memf, _rows(norm_mem_kv), w_kv)
        xf = _xattn(l, xf, kv, _rows(norm_pre_mem), w_q, w_xo, _rows(norm_post_mem), seq)

        xf = _ffn(l, xf, _rows(norm_pre_ffn), w_up, ffn_conv_w, _rows(ffn_conv_b), w_down,
                  _rows(norm_post_ffn), seq)
    return xf.reshape(batch, seq, D_MODEL)
```

```python
import functools

import jax
import jax.numpy as jnp
from jax import lax
from jax.experimental import pallas as pl
from jax.experimental.pallas import tpu as pltpu

F32 = jnp.float32
BF16 = jnp.bfloat16

D_MODEL = 1024
MEM_LEN = 256
EPS = 1e-6
GM_GROUPS = 8
GM_WIDTH = 1024
CHUNK = 128
SSD_INNER = 2048
SSD_HEAD_DIM = 64
SSD_HEADS = 32
SSD_GROUPS = 4
SSD_HPG = 8
SSD_STATE = 128
SSD_CONV = 4
SSD_XBC = 3072
SSD_BC = 2 * SSD_GROUPS * SSD_STATE
GROUP_COLS = SSD_INNER // SSD_GROUPS
X_HEADS = 4
X_HEAD_DIM = 256
FFN_DIM = 2816
FFN_CONV = 3
LANES = 128
MXU_COLS = 256
TAIL_ROWS = 8
MAIN_COLS = 9216
DT_OFF = 7168
SEG_OFF = (0, 1024, 2048, 4096, 7168, 9216)
STAGE_DEPTH = 2
FFN_STAGE_DEPTH = 4
MIX_CHUNKS = 4
ROW_SUB = 512
FFN_DOWN_GROUP = 4

VMEM_LIMIT = 56 * 1024 * 1024


def _params(*sem):
    return pltpu.CompilerParams(dimension_semantics=sem, vmem_limit_bytes=VMEM_LIMIT)


def _layer_spec(layer, *shape, resident=False):
    kw = dict(pipeline_mode=pl.Buffered(1)) if resident else {}
    return pl.BlockSpec((None,) + shape, lambda *_: (layer,) + (0,) * len(shape), **kw)


def _dot(a, b):
    return jnp.dot(a, b, preferred_element_type=F32)


def _dot_nt(a, b):
    return lax.dot_general(a, b, (((1,), (1,)), ((), ())), preferred_element_type=F32)


def _rms(x, g):
    ms = jnp.mean(x * x, axis=-1, keepdims=True)
    return x * lax.rsqrt(ms + EPS) * g


GELU_C1 = 0.7978845608028654
GELU_C2 = GELU_C1 * 0.044715


def _gelu(x):
    half = 0.5 * x
    return half * jnp.tanh(x * (GELU_C1 + GELU_C2 * (x * x))) + half


def _sigmoid(x):
    return 0.5 * jnp.tanh(0.5 * x) + 0.5


def _silu(x):
    half = 0.5 * x
    return half * jnp.tanh(half) + half


def _softplus(x):
    return jnp.maximum(x, 0.0) + jnp.log1p(jnp.exp(-jnp.abs(x)))


def _causal_conv(cur, tail, w, b):
    k_width = w.shape[0]
    row = lax.broadcasted_iota(jnp.int32, tail.shape, 0)
    y = b + w[k_width - 1:k_width, :] * cur
    for s in range(1, k_width):
        r = pltpu.roll(cur, s, axis=0)
        rt = pltpu.roll(tail, s, axis=0)
        head = jnp.where(row < s, rt, r[0:TAIL_ROWS, :])
        r = jnp.concatenate([head, r[TAIL_ROWS:, :]], axis=0)
        y = y + w[k_width - 1 - s:k_width - s, :] * r
    return y


def _interleave(a, b):
    out = []
    for k in range(max(len(a), len(b))):
        out.extend(a[k:k + 1])
        out.extend(b[k:k + 1])
    return out


def _staged(n, produce, consume, bufs, slot0):
    depth = len(bufs) - 1
    for k in range(n + depth):
        if k < n:
            bufs[k % len(bufs)][slot0] = produce(k)
        if k >= depth:
            consume(k - depth, bufs[(k - depth) % len(bufs)][slot0])


def _inproj_kernel(tiles_per_seq, x_ref, g_ref, w_ref, wg_ref, wdt_ref, vgain_ref, cw_ref, cb_ref,
                   o_ref, dt_ref, h_scr, v_scr, tail_scr, *stage):
    i = pl.program_id(0)
    tm = x_ref.shape[0]
    sub = MXU_COLS
    h = _rms(x_ref[...], g_ref[...]).astype(BF16)
    h_scr[...] = h
    dt_ref[...] = _dot(h, wdt_ref[...])

    @pl.when(i % tiles_per_seq == 0)
    def _():
        tail_scr[...] = jnp.zeros_like(tail_scr)

    ssq = []

    def ep_u(raw, c0):
        o_ref[:, c0:c0 + sub] = _gelu(raw).astype(BF16)

    def ep_v(raw, c0):
        lc = c0 - SEG_OFF[1]
        gv = _gelu(raw)
        v_scr[:, lc:lc + sub] = gv
        ssq.append(jnp.sum(gv * gv, axis=-1, keepdims=True))
        if c0 + sub == SEG_OFF[2]:
            scale = lax.rsqrt(sum(ssq) * (1.0 / GM_WIDTH) + EPS)
            for d0 in range(0, GM_WIDTH, sub):
                o_ref[:, SEG_OFF[1] + d0:SEG_OFF[1] + d0 + sub] = (
                    v_scr[:, d0:d0 + sub] * scale * vgain_ref[:, d0:d0 + sub]).astype(BF16)

    def ep_z(raw, c0):
        o_ref[:, c0:c0 + sub] = _silu(raw).astype(BF16)

    def ep_xbc(raw, c0):
        lc = c0 - SEG_OFF[3]
        y = _causal_conv(raw, tail_scr[:, lc:lc + sub], cw_ref[:, lc:lc + sub], cb_ref[:, lc:lc + sub])
        tail_scr[:, lc:lc + sub] = raw[tm - TAIL_ROWS:, :]
        o_ref[:, c0:c0 + sub] = _silu(y).astype(BF16)

    def ep_gate(raw, c0):
        o_ref[:, c0:c0 + sub] = _sigmoid(raw).astype(BF16)

    def seg_tasks(*pairs):
        return [(fn, c0) for seg, fn in pairs for c0 in range(SEG_OFF[seg], SEG_OFF[seg + 1], sub)]

    tasks = seg_tasks((1, ep_v)) + _interleave(seg_tasks((3, ep_xbc), (0, ep_u)),
                                               seg_tasks((2, ep_z), (4, ep_gate)))

    def produce(k):
        c0 = tasks[k][1]
        if c0 < SEG_OFF[4]:
            return _dot(h_scr[...], w_ref[:, c0:c0 + sub])
        return _dot(h_scr[...], wg_ref[:, c0 - SEG_OFF[4]:c0 - SEG_OFF[4] + sub])

    _staged(len(tasks), produce, lambda k, raw: tasks[k][0](raw, tasks[k][1]),
            stage, jnp.minimum(i, 0))


def _inproj(layer, x, g, w_front, w_gates, w_dt, v_gain, conv_w, conv_b, seq, tm=512):
    m = x.shape[0]
    return pl.pallas_call(
        functools.partial(_inproj_kernel, seq // tm),
        grid=(m // tm,),
        in_specs=[
            pl.BlockSpec((tm, D_MODEL), lambda i: (i, 0)),
            _layer_spec(layer, 1, D_MODEL),
            _layer_spec(layer, D_MODEL, SEG_OFF[4], resident=True),
            _layer_spec(layer, D_MODEL, MAIN_COLS - SEG_OFF[4], resident=True),
            _layer_spec(layer, D_MODEL, LANES, resident=True),
            _layer_spec(layer, 1, GM_WIDTH),
            _layer_spec(layer, SSD_CONV, SSD_XBC),
            _layer_spec(layer, 1, SSD_XBC),
        ],
        out_specs=[
            pl.BlockSpec((tm, MAIN_COLS), lambda i: (i, 0)),
            pl.BlockSpec((tm, LANES), lambda i: (i, 0)),
        ],
        out_shape=[
            jax.ShapeDtypeStruct((m, MAIN_COLS), BF16),
            jax.ShapeDtypeStruct((m, LANES), F32),
        ],
        scratch_shapes=[
            pltpu.VMEM((tm, D_MODEL), BF16),
            pltpu.VMEM((tm, GM_WIDTH), F32),
            pltpu.VMEM((TAIL_ROWS, SSD_XBC), F32),
        ] + [pltpu.VMEM((1, tm, MXU_COLS), F32)] * (STAGE_DEPTH + 1),
        compiler_params=_params("arbitrary"),
        name="inproj",
    )(x, g, w_front, w_gates, w_dt, v_gain, conv_w, conv_b)


def _kvproj_kernel(x_ref, g_ref, w_ref, o_ref):
    h = _rms(x_ref[...], g_ref[...]).astype(BF16)
    o_ref[...] = _dot(h, w_ref[...]).astype(BF16)


def _kvproj(layer, mem, g, w_kv, tm=512):
    m = mem.shape[0]
    n = 2 * D_MODEL
    return pl.pallas_call(
        _kvproj_kernel,
        grid=(m // tm,),
        in_specs=[
            pl.BlockSpec((tm, D_MODEL), lambda i: (i, 0)),
            _layer_spec(layer, 1, D_MODEL),
            _layer_spec(layer, D_MODEL, n),
        ],
        out_specs=pl.BlockSpec((tm, n), lambda i: (i, 0)),
        out_shape=jax.ShapeDtypeStruct((m, n), BF16),
        compiler_params=_params("arbitrary"),
        name="kvproj",
    )(mem, g, w_kv)


def _split2(x):
    hi = x.astype(BF16)
    lo = (x - hi.astype(F32)).astype(BF16)
    return hi, lo


def _split3(x):
    hi = x.astype(BF16)
    r = x - hi.astype(F32)
    mid = r.astype(BF16)
    lo = (r - mid.astype(F32)).astype(BF16)
    return hi, mid, lo


def _mixer_kernel(u_ref, v_ref, z_ref, xs_ref, bc_ref, dt_ref,
                  ws_ref, bexp_ref, dtb_ref, alog_ref, ddiag_ref, e_ref,
                  ya_ref, yb_ref,
                  state_scr, wsm_scr):
    c = pl.program_id(1)
    L = CHUNK
    rowi = lax.broadcasted_iota(jnp.int32, (L, L), 0)
    coli = lax.broadcasted_iota(jnp.int32, (L, L), 1)
    causal = rowi >= coli
    lane = lax.broadcasted_iota(jnp.int32, (L, LANES), 1)
    head_lane = lane < SSD_HEADS
    first_half = lane < SSD_HEAD_DIM
    zero_bf = jnp.zeros((L, LANES), BF16)
    tril = jnp.where(causal, 1.0, 0.0).astype(BF16)
    a_row = -jnp.exp(alog_ref[...])

    @pl.when(c == 0)
    def _():
        state_scr[...] = jnp.zeros_like(state_scr)
        for g in range(GM_GROUPS):
            wsm_scr[g] = jnp.where(causal, ws_ref[g], 0.0).astype(BF16)

    for ck in range(u_ref.shape[0] // L):
        rows = pl.ds(ck * L, L)
        for g in range(GM_GROUPS):
            sl = slice(g * LANES, (g + 1) * LANES)
            mixed = _dot(wsm_scr[g], v_ref[rows, sl]) + bexp_ref[:, sl]
            ya_ref[rows, sl] = u_ref[rows, sl] * mixed.astype(BF16)

        dt = jnp.where(head_lane, _softplus(dt_ref[rows, :] + dtb_ref[...]), 0.0)
        da = dt * a_row
        d_hi, d_mid, d_lo = _split3(da)
        a_cs = _dot(tril, d_hi) + _dot(tril, d_mid) + _dot(tril, d_lo)
        a_cs_t = a_cs.T
        dt_t = dt.T.astype(BF16)
        ea2 = jnp.concatenate(_split2(jnp.exp(a_cs)), axis=1)
        ws2 = jnp.concatenate(_split2(dt * jnp.exp(a_cs[L - 1:L, :] - a_cs)), axis=1)

        for g in range(SSD_GROUPS):
            gsl = slice(g * GROUP_COLS, (g + 1) * GROUP_COLS)
            bm_bf = bc_ref[rows, g * SSD_STATE:(g + 1) * SSD_STATE]
            cm_bf = bc_ref[rows, SSD_GROUPS * SSD_STATE + g * SSD_STATE:
                           SSD_GROUPS * SSD_STATE + (g + 1) * SSD_STATE]
            cb = _dot_nt(cm_bf, bm_bf).astype(BF16)
            xs_g_bf = xs_ref[rows, gsl]
            e_g = e_ref[:, gsl]
            ea_exp = _dot(ea2, e_g)
            wst_exp = _dot(ws2, e_g)
            state_g = state_scr[:, gsl]
            y_g = _dot(cm_bf, state_g.astype(BF16)) * ea_exp
            parts = []
            for pr in range(SSD_HPG // 2):
                sc = []
                for hh in range(2):
                    h = g * SSD_HPG + 2 * pr + hh
                    seg = a_cs[:, h:h + 1] - a_cs_t[h:h + 1, :]
                    lm = jnp.exp(jnp.where(causal, seg, -jnp.inf))
                    sc.append(cb * lm.astype(BF16) * dt_t[h:h + 1, :] + ddiag_ref[h])
                lhs = jnp.concatenate(sc, axis=1)
                x_pair = xs_g_bf[:, pr * LANES:(pr + 1) * LANES]
                rhs = jnp.concatenate([jnp.where(first_half, x_pair, zero_bf),
                                       jnp.where(first_half, zero_bf, x_pair)], axis=0)
                parts.append(_dot(lhs, rhs))
            y_g = y_g + jnp.concatenate(parts, axis=1)
            xw = xs_g_bf * wst_exp.astype(BF16)
            new_state = state_g * ea_exp[L - 1:L, :] + _dot(bm_bf.astype(F32).T.astype(BF16), xw)
            state_scr[:, gsl] = new_state
            y_g = y_g * z_ref[rows, gsl].astype(F32)
            ms = jnp.mean(y_g * y_g, axis=-1, keepdims=True)
            yb_ref[rows, gsl] = (y_g * lax.rsqrt(ms + EPS)).astype(BF16)


def _mixer(layer, proj, dt, w_s, b_exp, dt_bias, a_log, d_diag, e_mat, batch, seq):
    rows = MIX_CHUNKS * CHUNK
    steps = seq // rows
    row = lambda b, c: b * steps + c
    in_specs = [
        pl.BlockSpec((rows, GM_WIDTH), lambda b, c: (row(b, c), 0)),
        pl.BlockSpec((rows, GM_WIDTH), lambda b, c: (row(b, c), 1)),
        pl.BlockSpec((rows, SSD_INNER), lambda b, c: (row(b, c), 1)),
        pl.BlockSpec((rows, SSD_INNER), lambda b, c: (row(b, c), 2)),
        pl.BlockSpec((rows, SSD_BC), lambda b, c: (row(b, c), 6)),
        pl.BlockSpec((rows, LANES), lambda b, c: (row(b, c), 0)),
        _layer_spec(layer, GM_GROUPS, CHUNK, CHUNK),
        _layer_spec(layer, CHUNK, GM_WIDTH),
        _layer_spec(layer, 1, LANES),
        _layer_spec(layer, 1, LANES),
        _layer_spec(layer, SSD_HEADS, CHUNK, CHUNK),
        pl.BlockSpec((2 * LANES, SSD_INNER), lambda b, c: (0, 0)),
    ]
    m = batch * seq
    return pl.pallas_call(
        _mixer_kernel,
        grid=(batch, steps),
        in_specs=in_specs,
        out_specs=[
            pl.BlockSpec((rows, GM_WIDTH), lambda b, c: (row(b, c), 0)),
            pl.BlockSpec((rows, SSD_INNER), lambda b, c: (row(b, c), 0)),
        ],
        out_shape=[
            jax.ShapeDtypeStruct((m, GM_WIDTH), BF16),
            jax.ShapeDtypeStruct((m, SSD_INNER), BF16),
        ],
        scratch_shapes=[
            pltpu.VMEM((SSD_STATE, SSD_INNER), F32),
            pltpu.VMEM((GM_GROUPS, CHUNK, CHUNK), BF16),
        ],
        compiler_params=_params("arbitrary", "arbitrary"),
        name="mixer",
    )(proj, proj, proj, proj, proj, dt, w_s, b_exp, dt_bias, a_log, d_diag, e_mat)


def _merge_kernel(ya_ref, yb_ref, ga_ref, gb_ref, x_ref, wa_ref, wb_ref, wo_ref, g_ref, o_ref):
    for r0 in range(0, x_ref.shape[0], ROW_SUB):
        rows = pl.ds(r0, ROW_SUB)
        ya = _dot(ya_ref[rows, :], wa_ref[...])
        yb = _dot(yb_ref[rows, :], wb_ref[...])
        mixed = ga_ref[rows, :].astype(F32) * ya + gb_ref[rows, :].astype(F32) * yb
        y = _dot(mixed.astype(BF16), wo_ref[...])
        o_ref[rows, :] = x_ref[rows, :] + _rms(y, g_ref[...])


def _merge(layer, ya, yb, proj, x, w_a, w_b, w_o, g, tm=1024):
    m = x.shape[0]
    return pl.pallas_call(
        _merge_kernel,
        grid=(m // tm,),
        in_specs=[
            pl.BlockSpec((tm, GM_WIDTH), lambda i: (i, 0)),
            pl.BlockSpec((tm, SSD_INNER), lambda i: (i, 0)),
            pl.BlockSpec((tm, D_MODEL), lambda i: (i, 7)),
            pl.BlockSpec((tm, D_MODEL), lambda i: (i, 8)),
            pl.BlockSpec((tm, D_MODEL), lambda i: (i, 0)),
            _layer_spec(layer, GM_WIDTH, D_MODEL, resident=True),
            _layer_spec(layer, SSD_INNER, D_MODEL, resident=True),
            _layer_spec(layer, D_MODEL, D_MODEL, resident=True),
            _layer_spec(layer, 1, D_MODEL),
        ],
        out_specs=pl.BlockSpec((tm, D_MODEL), lambda i: (i, 0)),
        out_shape=jax.ShapeDtypeStruct((m, D_MODEL), F32),
        compiler_params=_params("arbitrary"),
        name="merge",
    )(ya, yb, proj, proj, x, w_a, w_b, w_o, g)


def _xattn_kernel(x_ref, gpre_ref, wq_ref, k_ref, v_ref, wo_ref, gpost_ref, o_ref):
    for r0 in range(0, x_ref.shape[0], ROW_SUB):
        rows = pl.ds(r0, ROW_SUB)
        x = x_ref[rows, :]
        h = _rms(x, gpre_ref[...]).astype(BF16)
        q = (_dot(h, wq_ref[...]) * (X_HEAD_DIM ** -0.5)).astype(BF16)
        outs = []
        for hd in range(X_HEADS):
            sl = slice(hd * X_HEAD_DIM, (hd + 1) * X_HEAD_DIM)
            s = _dot_nt(q[:, sl], k_ref[:, sl])
            e = jnp.exp(s - jnp.max(s, axis=-1, keepdims=True))
            p = e / jnp.sum(e, axis=-1, keepdims=True)
            outs.append(_dot(p.astype(BF16), v_ref[:, sl]).astype(BF16))
        o = jnp.concatenate(outs, axis=1)
        y = _dot(o, wo_ref[...])
        o_ref[rows, :] = x + _rms(y, gpost_ref[...])


def _xattn(layer, x, kv, g_pre, w_q, w_o, g_post, seq, tm=2048):
    m = x.shape[0]
    tiles = seq // tm
    return pl.pallas_call(
        _xattn_kernel,
        grid=(m // tm,),
        in_specs=[
            pl.BlockSpec((tm, D_MODEL), lambda i: (i, 0)),
            _layer_spec(layer, 1, D_MODEL),
            _layer_spec(layer, D_MODEL, D_MODEL),
            pl.BlockSpec((MEM_LEN, D_MODEL), lambda i: (i // tiles, 0)),
            pl.BlockSpec((MEM_LEN, D_MODEL), lambda i: (i // tiles, 1)),
            _layer_spec(layer, D_MODEL, D_MODEL),
            _layer_spec(layer, 1, D_MODEL),
        ],
        out_specs=pl.BlockSpec((tm, D_MODEL), lambda i: (i, 0)),
        out_shape=jax.ShapeDtypeStruct((m, D_MODEL), F32),
        compiler_params=_params("arbitrary"),
        name="xattn",
    )(x, g_pre, w_q, kv, kv, w_o, g_post)


def _ffn_kernel(tiles_per_seq, x_ref, gpre_ref, wup_ref, cw_ref, cb_ref, wd_ref, gpost_ref, o_ref,
                h_scr, g_scr, tail_scr, *stage):
    i = pl.program_id(0)
    tm = x_ref.shape[0]
    sub = MXU_COLS
    h_scr[...] = _rms(x_ref[...], gpre_ref[...]).astype(BF16)

    @pl.when(i % tiles_per_seq == 0)
    def _():
        tail_scr[...] = jnp.zeros_like(tail_scr)

    def produce(k):
        c0 = (k // 2) * sub + (k % 2) * FFN_DIM
        return _dot(h_scr[...], wup_ref[:, c0:c0 + sub])

    pending = {}
    down = []
    n_sub = FFN_DIM // sub

    def consume(k, raw):
        c0 = (k // 2) * sub + (k % 2) * FFN_DIM
        y = _causal_conv(raw, tail_scr[:, c0:c0 + sub], cw_ref[:, c0:c0 + sub], cb_ref[:, c0:c0 + sub])
        tail_scr[:, c0:c0 + sub] = raw[tm - TAIL_ROWS:, :]
        if k % 2 == 0:
            pending[k // 2] = _gelu(y)
            return
        j = k // 2
        g_scr[:, j * sub:(j + 1) * sub] = (pending.pop(j) * y).astype(BF16)
        if (j + 1) % FFN_DOWN_GROUP == 0 or j == n_sub - 1:
            lo = (j // FFN_DOWN_GROUP) * FFN_DOWN_GROUP * sub
            down.append(_dot(g_scr[:, lo:(j + 1) * sub], wd_ref[lo:(j + 1) * sub, :]))

    _staged(2 * n_sub, produce, consume, stage, jnp.minimum(i, 0))

    o_ref[...] = x_ref[...] + _rms(sum(down), gpost_ref[...])


def _ffn(layer, x, g_pre, w_up, conv_w, conv_b, w_down, g_post, seq, tm=512):
    m = x.shape[0]
    return pl.pallas_call(
        functools.partial(_ffn_kernel, seq // tm),
        grid=(m // tm,),
        in_specs=[
            pl.BlockSpec((tm, D_MODEL), lambda i: (i, 0)),
            _layer_spec(layer, 1, D_MODEL),
            _layer_spec(layer, D_MODEL, 2 * FFN_DIM, resident=True),
            _layer_spec(layer, FFN_CONV, 2 * FFN_DIM),
            _layer_spec(layer, 1, 2 * FFN_DIM),
            _layer_spec(layer, FFN_DIM, D_MODEL, resident=True),
            _layer_spec(layer, 1, D_MODEL),
        ],
        out_specs=pl.BlockSpec((tm, D_MODEL), lambda i: (i, 0)),
        out_shape=jax.ShapeDtypeStruct((m, D_MODEL), F32),
        scratch_shapes=[
            pltpu.VMEM((tm, D_MODEL), BF16),
            pltpu.VMEM((tm, FFN_DIM), BF16),
            pltpu.VMEM((TAIL_ROWS, 2 * FFN_DIM), F32),
        ] + [pltpu.VMEM((1, tm, MXU_COLS), F32)] * (FFN_STAGE_DEPTH + 1),
        compiler_params=_params("arbitrary"),
        name="ffn",
    )(x, g_pre, w_up, conv_w, conv_b, w_down, g_post)


def _rows(v):
    return v.reshape(v.shape[0], 1, v.shape[1])


def kernel(x, mem, norm_pre_mix, norm_post_mix, norm_pre_mem, norm_mem_kv, norm_post_mem,
           norm_pre_ffn, norm_post_ffn, w_in, gm_v_norm, gm_w_s, gm_b_s, ssd_conv_w, ssd_conv_b,
           ssd_dt_bias, ssd_a_log, ssd_d, ssd_norm, w_branch_a, w_branch_b, w_out,
           xa_w_q, xa_w_kv, xa_w_o, ffn_w_up, ffn_conv_w, ffn_conv_b, ffn_w_down):
    batch, seq, _ = x.shape
    depth = w_in.shape[0]
    xf = x.reshape(batch * seq, D_MODEL)
    memf = mem.reshape(batch * MEM_LEN, D_MODEL)

    w_front = w_in.astype(BF16)
    w_gates = w_front[:, :, DT_OFF + SSD_HEADS:]
    w_dt = jnp.pad(w_front[:, :, DT_OFF:DT_OFF + SSD_HEADS], ((0, 0), (0, 0), (0, LANES - SSD_HEADS)))
    pad_heads = lambda v: _rows(jnp.pad(v, ((0, 0), (0, LANES - SSD_HEADS))))
    b_exp = jnp.repeat(jnp.swapaxes(gm_b_s, 1, 2), LANES, axis=2)
    d_diag = (ssd_d[:, :, None, None] * jnp.eye(CHUNK, dtype=F32)).astype(BF16)
    e_mat = (lax.broadcasted_iota(jnp.int32, (2 * LANES, SSD_INNER), 0) % LANES
             == lax.broadcasted_iota(jnp.int32, (2 * LANES, SSD_INNER), 1) // SSD_HEAD_DIM).astype(BF16)
    bf = lambda w: w.astype(BF16)
    w_a, w_b, w_o = bf(w_branch_a), bf(w_branch_b * ssd_norm[:, :, None]), bf(w_out)
    w_q, w_kv, w_xo = bf(xa_w_q), bf(xa_w_kv), bf(xa_w_o)
    w_up, w_down = bf(ffn_w_up), bf(ffn_w_down)

    for l in range(depth):
        proj, dt = _inproj(l, xf, _rows(norm_pre_mix), w_front, w_gates, w_dt, _rows(gm_v_norm),
                           ssd_conv_w, _rows(ssd_conv_b), seq)
        ya, yb = _mixer(l, proj, dt, gm_w_s, b_exp, pad_heads(ssd_dt_bias), pad_heads(ssd_a_log),
                        d_diag, e_mat, batch, seq)
        xf = _merge(l, ya, yb, proj, xf, w_a, w_b, w_o, _rows(norm_post_mix))

        kv = _kvproj(l, memf, _rows(norm_mem_kv), w_kv)
        xf = _xattn(l, xf, kv, _rows(norm_pre_mem), w_q, w_xo, _rows(norm_post_mem), seq)

        xf = _ffn(l, xf, _rows(norm_pre_ffn), w_up, ffn_conv_w, _rows(ffn_conv_b), w_down,
                  _rows(norm_post_ffn), seq)
    return xf.reshape(batch, seq, D_MODEL)
```

```python
import functools

import jax
import jax.numpy as jnp
from jax import lax
from jax.experimental import pallas as pl
from jax.experimental.pallas import tpu as pltpu

F32 = jnp.float32
BF16 = jnp.bfloat16

D_MODEL = 1024
MEM_LEN = 256
EPS = 1e-6
GM_GROUPS = 8
GM_WIDTH = 1024
CHUNK = 128
SSD_INNER = 2048
SSD_HEAD_DIM = 64
SSD_HEADS = 32
SSD_GROUPS = 4
SSD_HPG = 8
SSD_STATE = 128
SSD_CONV = 4
SSD_XBC = 3072
SSD_BC = 2 * SSD_GROUPS * SSD_STATE
GROUP_COLS = SSD_INNER // SSD_GROUPS
X_HEADS = 4
X_HEAD_DIM = 256
FFN_DIM = 2816
FFN_CONV = 3
LANES = 128
MXU_COLS = 256
TAIL_ROWS = 8
MAIN_COLS = 9216
DT_OFF = 7168
SEG_OFF = (0, 1024, 2048, 4096, 7168, 9216)
STAGE_DEPTH = 2
FFN_STAGE_DEPTH = 4
MIX_CHUNKS = 4
ROW_SUB = 512
FFN_DOWN_GROUP = 4

VMEM_LIMIT = 56 * 1024 * 1024


def _params(*sem):
    return pltpu.CompilerParams(dimension_semantics=sem, vmem_limit_bytes=VMEM_LIMIT)


def _layer_spec(layer, *shape, resident=False):
    kw = dict(pipeline_mode=pl.Buffered(1)) if resident else {}
    return pl.BlockSpec((None,) + shape, lambda *_: (layer,) + (0,) * len(shape), **kw)


def _dot(a, b):
    return jnp.dot(a, b, preferred_element_type=F32)


def _dot_nt(a, b):
    return lax.dot_general(a, b, (((1,), (1,)), ((), ())), preferred_element_type=F32)


def _rms(x, g):
    ms = jnp.mean(x * x, axis=-1, keepdims=True)
    return x * lax.rsqrt(ms + EPS) * g


GELU_C1 = 0.7978845608028654
GELU_C2 = GELU_C1 * 0.044715


def _gelu(x):
    half = 0.5 * x
    return half * jnp.tanh(x * (GELU_C1 + GELU_C2 * (x * x))) + half


def _sigmoid(x):
    return 0.5 * jnp.tanh(0.5 * x) + 0.5


def _silu(x):
    half = 0.5 * x
    return half * jnp.tanh(half) + half


def _softplus(x):
    return jnp.maximum(x, 0.0) + jnp.log1p(jnp.exp(-jnp.abs(x)))


def _causal_conv(cur, tail, w, b):
    k_width = w.shape[0]
    row = lax.broadcasted_iota(jnp.int32, tail.shape, 0)
    y = b + w[k_width - 1:k_width, :] * cur
    for s in range(1, k_width):
        r = pltpu.roll(cur, s, axis=0)
        rt = pltpu.roll(tail, s, axis=0)
        head = jnp.where(row < s, rt, r[0:TAIL_ROWS, :])
        r = jnp.concatenate([head, r[TAIL_ROWS:, :]], axis=0)
        y = y + w[k_width - 1 - s:k_width - s, :] * r
    return y


def _interleave(a, b):
    out = []
    for k in range(max(len(a), len(b))):
        out.extend(a[k:k + 1])
        out.extend(b[k:k + 1])
    return out


def _staged(n, produce, consume, bufs, slot0):
    depth = len(bufs) - 1
    for k in range(n + depth):
        if k < n:
            bufs[k % len(bufs)][slot0] = produce(k)
        if k >= depth:
            consume(k - depth, bufs[(k - depth) % len(bufs)][slot0])


def _inproj_kernel(tiles_per_seq, x_ref, g_ref, w_ref, wg_ref, wdt_ref, vgain_ref, cw_ref, cb_ref,
                   o_ref, dt_ref, h_scr, v_scr, tail_scr, *stage):
    i = pl.program_id(0)
    tm = x_ref.shape[0]
    sub = MXU_COLS
    h = _rms(x_ref[...], g_ref[...]).astype(BF16)
    h_scr[...] = h
    dt_ref[...] = _dot(h, wdt_ref[...])

    @pl.when(i % tiles_per_seq == 0)
    def _():
        tail_scr[...] = jnp.zeros_like(tail_scr)

    ssq = []

    def ep_u(raw, c0):
        o_ref[:, c0:c0 + sub] = _gelu(raw).astype(BF16)

    def ep_v(raw, c0):
        lc = c0 - SEG_OFF[1]
        gv = _gelu(raw)
        v_scr[:, lc:lc + sub] = gv
        ssq.append(jnp.sum(gv * gv, axis=-1, keepdims=True))
        if c0 + sub == SEG_OFF[2]:
            scale = lax.rsqrt(sum(ssq) * (1.0 / GM_WIDTH) + EPS)
            for d0 in range(0, GM_WIDTH, sub):
                o_ref[:, SEG_OFF[1] + d0:SEG_OFF[1] + d0 + sub] = (
                    v_scr[:, d0:d0 + sub] * scale * vgain_ref[:, d0:d0 + sub]).astype(BF16)

    def ep_z(raw, c0):
        o_ref[:, c0:c0 + sub] = _silu(raw).astype(BF16)

    def ep_xbc(raw, c0):
        lc = c0 - SEG_OFF[3]
        y = _causal_conv(raw, tail_scr[:, lc:lc + sub], cw_ref[:, lc:lc + sub], cb_ref[:, lc:lc + sub])
        tail_scr[:, lc:lc + sub] = raw[tm - TAIL_ROWS:, :]
        o_ref[:, c0:c0 + sub] = _silu(y.astype(BF16))

    def ep_gate(raw, c0):
        o_ref[:, c0:c0 + sub] = _sigmoid(raw).astype(BF16)

    def seg_tasks(*pairs):
        return [(fn, c0) for seg, fn in pairs for c0 in range(SEG_OFF[seg], SEG_OFF[seg + 1], sub)]

    tasks = seg_tasks((1, ep_v)) + _interleave(seg_tasks((3, ep_xbc), (0, ep_u)),
                                               seg_tasks((2, ep_z), (4, ep_gate)))

    def produce(k):
        c0 = tasks[k][1]
        if c0 < SEG_OFF[4]:
            return _dot(h_scr[...], w_ref[:, c0:c0 + sub])
        return _dot(h_scr[...], wg_ref[:, c0 - SEG_OFF[4]:c0 - SEG_OFF[4] + sub])

    _staged(len(tasks), produce, lambda k, raw: tasks[k][0](raw, tasks[k][1]),
            stage, jnp.minimum(i, 0))


def _inproj(layer, x, g, w_front, w_gates, w_dt, v_gain, conv_w, conv_b, seq, tm=512):
    m = x.shape[0]
    return pl.pallas_call(
        functools.partial(_inproj_kernel, seq // tm),
        grid=(m // tm,),
        in_specs=[
            pl.BlockSpec((tm, D_MODEL), lambda i: (i, 0)),
            _layer_spec(layer, 1, D_MODEL),
            _layer_spec(layer, D_MODEL, SEG_OFF[4], resident=True),
            _layer_spec(layer, D_MODEL, MAIN_COLS - SEG_OFF[4], resident=True),
            _layer_spec(layer, D_MODEL, LANES, resident=True),
            _layer_spec(layer, 1, GM_WIDTH),
            _layer_spec(layer, SSD_CONV, SSD_XBC),
            _layer_spec(layer, 1, SSD_XBC),
        ],
        out_specs=[
            pl.BlockSpec((tm, MAIN_COLS), lambda i: (i, 0)),
            pl.BlockSpec((tm, LANES), lambda i: (i, 0)),
        ],
        out_shape=[
            jax.ShapeDtypeStruct((m, MAIN_COLS), BF16),
            jax.ShapeDtypeStruct((m, LANES), F32),
        ],
        scratch_shapes=[
            pltpu.VMEM((tm, D_MODEL), BF16),
            pltpu.VMEM((tm, GM_WIDTH), F32),
            pltpu.VMEM((TAIL_ROWS, SSD_XBC), F32),
        ] + [pltpu.VMEM((1, tm, MXU_COLS), F32)] * (STAGE_DEPTH + 1),
        compiler_params=_params("arbitrary"),
        name="inproj",
    )(x, g, w_front, w_gates, w_dt, v_gain, conv_w, conv_b)


def _kvproj_kernel(x_ref, g_ref, w_ref, o_ref):
    h = _rms(x_ref[...], g_ref[...]).astype(BF16)
    o_ref[...] = _dot(h, w_ref[...]).astype(BF16)


def _kvproj(layer, mem, g, w_kv, tm=512):
    m = mem.shape[0]
    n = 2 * D_MODEL
    return pl.pallas_call(
        _kvproj_kernel,
        grid=(m // tm,),
        in_specs=[
            pl.BlockSpec((tm, D_MODEL), lambda i: (i, 0)),
            _layer_spec(layer, 1, D_MODEL),
            _layer_spec(layer, D_MODEL, n),
        ],
        out_specs=pl.BlockSpec((tm, n), lambda i: (i, 0)),
        out_shape=jax.ShapeDtypeStruct((m, n), BF16),
        compiler_params=_params("arbitrary"),
        name="kvproj",
    )(mem, g, w_kv)


def _split2(x):
    hi = x.astype(BF16)
    lo = (x - hi.astype(F32)).astype(BF16)
    return hi, lo


def _split3(x):
    hi = x.astype(BF16)
    r = x - hi.astype(F32)
    mid = r.astype(BF16)
    lo = (r - mid.astype(F32)).astype(BF16)
    return hi, mid, lo


def _mixer_kernel(u_ref, v_ref, z_ref, xs_ref, bc_ref, dt_ref,
                  ws_ref, bexp_ref, dtb_ref, alog_ref, ddiag_ref, e_ref,
                  ya_ref, yb_ref,
                  state_scr, wsm_scr):
    c = pl.program_id(1)
    L = CHUNK
    rowi = lax.broadcasted_iota(jnp.int32, (L, L), 0)
    coli = lax.broadcasted_iota(jnp.int32, (L, L), 1)
    causal = rowi >= coli
    lane = lax.broadcasted_iota(jnp.int32, (L, LANES), 1)
    head_lane = lane < SSD_HEADS
    first_half = lane < SSD_HEAD_DIM
    zero_bf = jnp.zeros((L, LANES), BF16)
    tril = jnp.where(causal, 1.0, 0.0).astype(BF16)
    a_row = -jnp.exp(alog_ref[...])

    @pl.when(c == 0)
    def _():
        state_scr[...] = jnp.zeros_like(state_scr)
        for g in range(GM_GROUPS):
            wsm_scr[g] = jnp.where(causal, ws_ref[g], 0.0).astype(BF16)

    for ck in range(u_ref.shape[0] // L):
        rows = pl.ds(ck * L, L)
        for g in range(GM_GROUPS):
            sl = slice(g * LANES, (g + 1) * LANES)
            mixed = _dot(wsm_scr[g], v_ref[rows, sl]) + bexp_ref[:, sl]
            ya_ref[rows, sl] = u_ref[rows, sl] * mixed.astype(BF16)

        dt = jnp.where(head_lane, _softplus(dt_ref[rows, :] + dtb_ref[...]), 0.0)
        da = dt * a_row
        d_hi, d_mid, d_lo = _split3(da)
        a_cs = _dot(tril, d_hi) + _dot(tril, d_mid) + _dot(tril, d_lo)
        a_cs_t = a_cs.T
        dt_t = dt.T.astype(BF16)
        ea2 = jnp.concatenate(_split2(jnp.exp(a_cs)), axis=1)
        ws2 = jnp.concatenate(_split2(dt * jnp.exp(a_cs[L - 1:L, :] - a_cs)), axis=1)

        for g in range(SSD_GROUPS):
            gsl = slice(g * GROUP_COLS, (g + 1) * GROUP_COLS)
            bm_bf = bc_ref[rows, g * SSD_STATE:(g + 1) * SSD_STATE]
            cm_bf = bc_ref[rows, SSD_GROUPS * SSD_STATE + g * SSD_STATE:
                           SSD_GROUPS * SSD_STATE + (g + 1) * SSD_STATE]
            cb = _dot_nt(cm_bf, bm_bf).astype(BF16)
            xs_g_bf = xs_ref[rows, gsl]
            e_g = e_ref[:, gsl]
            ea_exp = _dot(ea2, e_g)
            wst_exp = _dot(ws2, e_g)
            state_g = state_scr[:, gsl]
            y_g = _dot(cm_bf, state_g.astype(BF16)) * ea_exp
            parts = []
            for pr in range(SSD_HPG // 2):
                sc = []
                for hh in range(2):
                    h = g * SSD_HPG + 2 * pr + hh
                    seg = a_cs[:, h:h + 1] - a_cs_t[h:h + 1, :]
                    lm = jnp.exp(jnp.where(causal, seg, -jnp.inf))
                    sc.append(cb * lm.astype(BF16) * dt_t[h:h + 1, :] + ddiag_ref[h])
                lhs = jnp.concatenate(sc, axis=1)
                x_pair = xs_g_bf[:, pr * LANES:(pr + 1) * LANES]
                rhs = jnp.concatenate([jnp.where(first_half, x_pair, zero_bf),
                                       jnp.where(first_half, zero_bf, x_pair)], axis=0)
                parts.append(_dot(lhs, rhs))
            y_g = y_g + jnp.concatenate(parts, axis=1)
            xw = xs_g_bf * wst_exp.astype(BF16)
            new_state = state_g * ea_exp[L - 1:L, :] + _dot(bm_bf.astype(F32).T.astype(BF16), xw)
            state_scr[:, gsl] = new_state
            y_g = y_g * z_ref[rows, gsl].astype(F32)
            ms = jnp.mean(y_g * y_g, axis=-1, keepdims=True)
            yb_ref[rows, gsl] = (y_g * lax.rsqrt(ms + EPS)).astype(BF16)


def _mixer(layer, proj, dt, w_s, b_exp, dt_bias, a_log, d_diag, e_mat, batch, seq):
    rows = MIX_CHUNKS * CHUNK
    steps = seq // rows
    row = lambda b, c: b * steps + c
    in_specs = [
        pl.BlockSpec((rows, GM_WIDTH), lambda b, c: (row(b, c), 0)),
        pl.BlockSpec((rows, GM_WIDTH), lambda b, c: (row(b, c), 1)),
        pl.BlockSpec((rows, SSD_INNER), lambda b, c: (row(b, c), 1)),
        pl.BlockSpec((rows, SSD_INNER), lambda b, c: (row(b, c), 2)),
        pl.BlockSpec((rows, SSD_BC), lambda b, c: (row(b, c), 6)),
        pl.BlockSpec((rows, LANES), lambda b, c: (row(b, c), 0)),
        _layer_spec(layer, GM_GROUPS, CHUNK, CHUNK),
        _layer_spec(layer, CHUNK, GM_WIDTH),
        _layer_spec(layer, 1, LANES),
        _layer_spec(layer, 1, LANES),
        _layer_spec(layer, SSD_HEADS, CHUNK, CHUNK),
        pl.BlockSpec((2 * LANES, SSD_INNER), lambda b, c: (0, 0)),
    ]
    m = batch * seq
    return pl.pallas_call(
        _mixer_kernel,
        grid=(batch, steps),
        in_specs=in_specs,
        out_specs=[
            pl.BlockSpec((rows, GM_WIDTH), lambda b, c: (row(b, c), 0)),
            pl.BlockSpec((rows, SSD_INNER), lambda b, c: (row(b, c), 0)),
        ],
        out_shape=[
            jax.ShapeDtypeStruct((m, GM_WIDTH), BF16),
            jax.ShapeDtypeStruct((m, SSD_INNER), BF16),
        ],
        scratch_shapes=[
            pltpu.VMEM((SSD_STATE, SSD_INNER), F32),
            pltpu.VMEM((GM_GROUPS, CHUNK, CHUNK), BF16),
        ],
        compiler_params=_params("arbitrary", "arbitrary"),
        name="mixer",
    )(proj, proj, proj, proj, proj, dt, w_s, b_exp, dt_bias, a_log, d_diag, e_mat)


def _merge_kernel(ya_ref, yb_ref, ga_ref, gb_ref, x_ref, wa_ref, wb_ref, wo_ref, g_ref, o_ref):
    for r0 in range(0, x_ref.shape[0], ROW_SUB):
        rows = pl.ds(r0, ROW_SUB)
        ya = _dot(ya_ref[rows, :], wa_ref[...])
        yb = _dot(yb_ref[rows, :], wb_ref[...])
        mixed = ga_ref[rows, :].astype(F32) * ya + gb_ref[rows, :].astype(F32) * yb
        y = _dot(mixed.astype(BF16), wo_ref[...])
        o_ref[rows, :] = x_ref[rows, :] + _rms(y, g_ref[...])


def _merge(layer, ya, yb, proj, x, w_a, w_b, w_o, g, tm=1024):
    m = x.shape[0]
    return pl.pallas_call(
        _merge_kernel,
        grid=(m // tm,),
        in_specs=[
            pl.BlockSpec((tm, GM_WIDTH), lambda i: (i, 0)),
            pl.BlockSpec((tm, SSD_INNER), lambda i: (i, 0)),
            pl.BlockSpec((tm, D_MODEL), lambda i: (i, 7)),
            pl.BlockSpec((tm, D_MODEL), lambda i: (i, 8)),
            pl.BlockSpec((tm, D_MODEL), lambda i: (i, 0)),
            _layer_spec(layer, GM_WIDTH, D_MODEL, resident=True),
            _layer_spec(layer, SSD_INNER, D_MODEL, resident=True),
            _layer_spec(layer, D_MODEL, D_MODEL, resident=True),
            _layer_spec(layer, 1, D_MODEL),
        ],
        out_specs=pl.BlockSpec((tm, D_MODEL), lambda i: (i, 0)),
        out_shape=jax.ShapeDtypeStruct((m, D_MODEL), F32),
        compiler_params=_params("arbitrary"),
        name="merge",
    )(ya, yb, proj, proj, x, w_a, w_b, w_o, g)


def _xattn_kernel(x_ref, gpre_ref, wq_ref, k_ref, v_ref, wo_ref, gpost_ref, o_ref):
    for r0 in range(0, x_ref.shape[0], ROW_SUB):
        rows = pl.ds(r0, ROW_SUB)
        x = x_ref[rows, :]
        h = _rms(x, gpre_ref[...]).astype(BF16)
        q = (_dot(h, wq_ref[...]) * (X_HEAD_DIM ** -0.5)).astype(BF16)
        outs = []
        for hd in range(X_HEADS):
            sl = slice(hd * X_HEAD_DIM, (hd + 1) * X_HEAD_DIM)
            s = _dot_nt(q[:, sl], k_ref[:, sl])
            e = jnp.exp(s - jnp.max(s, axis=-1, keepdims=True))
            p = e / jnp.sum(e, axis=-1, keepdims=True)
            outs.append(_dot(p.astype(BF16), v_ref[:, sl]).astype(BF16))
        o = jnp.concatenate(outs, axis=1)
        y = _dot(o, wo_ref[...])
        o_ref[rows, :] = x + _rms(y, gpost_ref[...])


def _xattn(layer, x, kv, g_pre, w_q, w_o, g_post, seq, tm=2048):
    m = x.shape[0]
    tiles = seq // tm
    return pl.pallas_call(
        _xattn_kernel,
        grid=(m // tm,),
        in_specs=[
            pl.BlockSpec((tm, D_MODEL), lambda i: (i, 0)),
            _layer_spec(layer, 1, D_MODEL),
            _layer_spec(layer, D_MODEL, D_MODEL),
            pl.BlockSpec((MEM_LEN, D_MODEL), lambda i: (i // tiles, 0)),
            pl.BlockSpec((MEM_LEN, D_MODEL), lambda i: (i // tiles, 1)),
            _layer_spec(layer, D_MODEL, D_MODEL),
            _layer_spec(layer, 1, D_MODEL),
        ],
        out_specs=pl.BlockSpec((tm, D_MODEL), lambda i: (i, 0)),
        out_shape=jax.ShapeDtypeStruct((m, D_MODEL), F32),
        compiler_params=_params("arbitrary"),
        name="xattn",
    )(x, g_pre, w_q, kv, kv, w_o, g_post)


def _ffn_kernel(tiles_per_seq, x_ref, gpre_ref, wup_ref, cw_ref, cb_ref, wd_ref, gpost_ref, o_ref,
                h_scr, g_scr, tail_scr, *stage):
    i = pl.program_id(0)
    tm = x_ref.shape[0]
    sub = MXU_COLS
    h_scr[...] = _rms(x_ref[...], gpre_ref[...]).astype(BF16)

    @pl.when(i % tiles_per_seq == 0)
    def _():
        tail_scr[...] = jnp.zeros_like(tail_scr)

    def produce(k):
        c0 = (k // 2) * sub + (k % 2) * FFN_DIM
        return _dot(h_scr[...], wup_ref[:, c0:c0 + sub])

    pending = {}
    down = []
    n_sub = FFN_DIM // sub

    def consume(k, raw):
        c0 = (k // 2) * sub + (k % 2) * FFN_DIM
        y = _causal_conv(raw, tail_scr[:, c0:c0 + sub], cw_ref[:, c0:c0 + sub], cb_ref[:, c0:c0 + sub])
        tail_scr[:, c0:c0 + sub] = raw[tm - TAIL_ROWS:, :]
        if k % 2 == 0:
            pending[k // 2] = _gelu(y)
            return
        j = k // 2
        g_scr[:, j * sub:(j + 1) * sub] = (pending.pop(j) * y).astype(BF16)
        if (j + 1) % FFN_DOWN_GROUP == 0 or j == n_sub - 1:
            lo = (j // FFN_DOWN_GROUP) * FFN_DOWN_GROUP * sub
            down.append(_dot(g_scr[:, lo:(j + 1) * sub], wd_ref[lo:(j + 1) * sub, :]))

    _staged(2 * n_sub, produce, consume, stage, jnp.minimum(i, 0))

    o_ref[...] = x_ref[...] + _rms(sum(down), gpost_ref[...])


def _ffn(layer, x, g_pre, w_up, conv_w, conv_b, w_down, g_post, seq, tm=512):
    m = x.shape[0]
    return pl.pallas_call(
        functools.partial(_ffn_kernel, seq // tm),
        grid=(m // tm,),
        in_specs=[
            pl.BlockSpec((tm, D_MODEL), lambda i: (i, 0)),
            _layer_spec(layer, 1, D_MODEL),
            _layer_spec(layer, D_MODEL, 2 * FFN_DIM, resident=True),
            _layer_spec(layer, FFN_CONV, 2 * FFN_DIM),
            _layer_spec(layer, 1, 2 * FFN_DIM),
            _layer_spec(layer, FFN_DIM, D_MODEL, resident=True),
            _layer_spec(layer, 1, D_MODEL),
        ],
        out_specs=pl.BlockSpec((tm, D_MODEL), lambda i: (i, 0)),
        out_shape=jax.ShapeDtypeStruct((m, D_MODEL), F32),
        scratch_shapes=[
            pltpu.VMEM((tm, D_MODEL), BF16),
            pltpu.VMEM((tm, FFN_DIM), BF16),
            pltpu.VMEM((TAIL_ROWS, 2 * FFN_DIM), F32),
        ] + [pltpu.VMEM((1, tm, MXU_COLS), F32)] * (FFN_STAGE_DEPTH + 1),
        compiler_params=_params("arbitrary"),
        name="ffn",
    )(x, g_pre, w_up, conv_w, conv_b, w_down, g_post)


def _rows(v):
    return v.reshape(v.shape[0], 1, v.shape[1])


def kernel(x, mem, norm_pre_mix, norm_post_mix, norm_pre_mem, norm_mem_kv, norm_post_mem,
           norm_pre_ffn, norm_post_ffn, w_in, gm_v_norm, gm_w_s, gm_b_s, ssd_conv_w, ssd_conv_b,
           ssd_dt_bias, ssd_a_log, ssd_d, ssd_norm, w_branch_a, w_branch_b, w_out,
           xa_w_q, xa_w_kv, xa_w_o, ffn_w_up, ffn_conv_w, ffn_conv_b, ffn_w_down):
    batch, seq, _ = x.shape
    depth = w_in.shape[0]
    xf = x.reshape(batch * seq, D_MODEL)
    memf = mem.reshape(batch * MEM_LEN, D_MODEL)

    w_front = w_in.astype(BF16)
    w_gates = w_front[:, :, DT_OFF + SSD_HEADS:]
    w_dt = jnp.pad(w_front[:, :, DT_OFF:DT_OFF + SSD_HEADS], ((0, 0), (0, 0), (0, LANES - SSD_HEADS)))
    pad_heads = lambda v: _rows(jnp.pad(v, ((0, 0), (0, LANES - SSD_HEADS))))
    b_exp = jnp.repeat(jnp.swapaxes(gm_b_s, 1, 2), LANES, axis=2)
    d_diag = (ssd_d[:, :, None, None] * jnp.eye(CHUNK, dtype=F32)).astype(BF16)
    e_mat = (lax.broadcasted_iota(jnp.int32, (2 * LANES, SSD_INNER), 0) % LANES
             == lax.broadcasted_iota(jnp.int32, (2 * LANES, SSD_INNER), 1) // SSD_HEAD_DIM).astype(BF16)
    bf = lambda w: w.astype(BF16)
    w_a, w_b, w_o = bf(w_branch_a), bf(w_branch_b * ssd_norm[:, :, None]), bf(w_out)
    w_q, w_kv, w_xo = bf(xa_w_q), bf(xa_w_kv), bf(xa_w_o)
    w_up, w_down = bf(ffn_w_up), bf(ffn_w_down)

    for l in range(depth):
        proj, dt = _inproj(l, xf, _rows(norm_pre_mix), w_front, w_gates, w_dt, _rows(gm_v_norm),
                           ssd_conv_w, _rows(ssd_conv_b), seq)
        ya, yb = _mixer(l, proj, dt, gm_w_s, b_exp, pad_heads(ssd_dt_bias), pad_heads(ssd_a_log),
                        d_diag, e_mat, batch, seq)
        xf = _merge(l, ya, yb, proj, xf, w_a, w_b, w_o, _rows(norm_post_mix))

        kv = _kvproj(l, memf, _rows(norm_mem_kv), w_kv)
        xf = _xattn(l, xf, kv, _rows(norm_pre_mem), w_q, w_xo, _rows(norm_post_mem), seq)

        xf = _ffn(l, xf, _rows(norm_pre_ffn), w_up, ffn_conv_w, _rows(ffn_conv_b), w_down,
                  _rows(norm_post_ffn), seq)
    return xf.reshape(batch, seq, D_MODEL)
```

```python
import functools

import jax
import jax.numpy as jnp
from jax import lax
from jax.experimental import pallas as pl
from jax.experimental.pallas import tpu as pltpu

F32 = jnp.float32
BF16 = jnp.bfloat16

D_MODEL = 1024
MEM_LEN = 256
EPS = 1e-6
GM_GROUPS = 8
GM_WIDTH = 1024
CHUNK = 128
SSD_INNER = 2048
SSD_HEAD_DIM = 64
SSD_HEADS = 32
SSD_GROUPS = 4
SSD_HPG = 8
SSD_STATE = 128
SSD_CONV = 4
SSD_XBC = 3072
SSD_BC = 2 * SSD_GROUPS * SSD_STATE
GROUP_COLS = SSD_INNER // SSD_GROUPS
X_HEADS = 4
X_HEAD_DIM = 256
FFN_DIM = 2816
FFN_CONV = 3
LANES = 128
MXU_COLS = 256
TAIL_ROWS = 8
MAIN_COLS = 9216
DT_OFF = 7168
SEG_OFF = (0, 1024, 2048, 4096, 7168, 9216)
STAGE_DEPTH = 2
FFN_STAGE_DEPTH = 4
MIX_CHUNKS = 4
ROW_SUB = 512
FFN_DOWN_GROUP = 4

VMEM_LIMIT = 56 * 1024 * 1024


def _params(*sem):
    return pltpu.CompilerParams(dimension_semantics=sem, vmem_limit_bytes=VMEM_LIMIT)


def _layer_spec(layer, *shape, resident=False):
    kw = dict(pipeline_mode=pl.Buffered(1)) if resident else {}
    return pl.BlockSpec((None,) + shape, lambda *_: (layer,) + (0,) * len(shape), **kw)


def _dot(a, b):
    return jnp.dot(a, b, preferred_element_type=F32)


def _dot_nt(a, b):
    return lax.dot_general(a, b, (((1,), (1,)), ((), ())), preferred_element_type=F32)


def _rms(x, g):
    ms = jnp.mean(x * x, axis=-1, keepdims=True)
    return x * lax.rsqrt(ms + EPS) * g


GELU_C1 = 0.7978845608028654
GELU_C2 = GELU_C1 * 0.044715


def _gelu(x):
    half = 0.5 * x
    return half * jnp.tanh(x * (GELU_C1 + GELU_C2 * (x * x))) + half


def _sigmoid(x):
    return 0.5 * jnp.tanh(0.5 * x) + 0.5


def _silu(x):
    half = 0.5 * x
    return half * jnp.tanh(half) + half


def _softplus(x):
    return jnp.maximum(x, 0.0) + jnp.log1p(jnp.exp(-jnp.abs(x)))


def _causal_conv(cur, tail, w, b):
    k_width = w.shape[0]
    row = lax.broadcasted_iota(jnp.int32, tail.shape, 0)
    acc = w[0:1, :] * cur
    acc_tail = w[0:1, :] * tail
    for k in range(1, k_width):
        r = pltpu.roll(acc, 1, axis=0)
        rt = pltpu.roll(acc_tail, 1, axis=0)
        head = jnp.where(row < 1, rt, r[0:TAIL_ROWS, :])
        r = jnp.concatenate([head, r[TAIL_ROWS:, :]], axis=0)
        acc = w[k:k + 1, :] * cur + r
        acc_tail = w[k:k + 1, :] * tail + rt
    return acc + b


def _interleave(a, b):
    out = []
    for k in range(max(len(a), len(b))):
        out.extend(a[k:k + 1])
        out.extend(b[k:k + 1])
    return out


def _staged(n, produce, consume, bufs, slot0):
    depth = len(bufs) - 1
    for k in range(n + depth):
        if k < n:
            bufs[k % len(bufs)][slot0] = produce(k)
        if k >= depth:
            consume(k - depth, bufs[(k - depth) % len(bufs)][slot0])


def _inproj_kernel(tiles_per_seq, x_ref, g_ref, w_ref, wg_ref, wdt_ref, vgain_ref, cw_ref, cb_ref,
                   o_ref, dt_ref, h_scr, v_scr, tail_scr, *stage):
    i = pl.program_id(0)
    tm = x_ref.shape[0]
    sub = MXU_COLS
    h = _rms(x_ref[...], g_ref[...]).astype(BF16)
    h_scr[...] = h
    dt_ref[...] = _dot(h, wdt_ref[...])

    @pl.when(i % tiles_per_seq == 0)
    def _():
        tail_scr[...] = jnp.zeros_like(tail_scr)

    ssq = []

    def ep_u(raw, c0):
        o_ref[:, c0:c0 + sub] = _gelu(raw).astype(BF16)

    def ep_v(raw, c0):
        lc = c0 - SEG_OFF[1]
        gv = _gelu(raw)
        v_scr[:, lc:lc + sub] = gv
        ssq.append(jnp.sum(gv * gv, axis=-1, keepdims=True))
        if c0 + sub == SEG_OFF[2]:
            scale = lax.rsqrt(sum(ssq) * (1.0 / GM_WIDTH) + EPS)
            for d0 in range(0, GM_WIDTH, sub):
                o_ref[:, SEG_OFF[1] + d0:SEG_OFF[1] + d0 + sub] = (
                    v_scr[:, d0:d0 + sub] * scale * vgain_ref[:, d0:d0 + sub]).astype(BF16)

    def ep_z(raw, c0):
        o_ref[:, c0:c0 + sub] = _silu(raw).astype(BF16)

    def ep_xbc(raw, c0):
        lc = c0 - SEG_OFF[3]
        y = _causal_conv(raw, tail_scr[:, lc:lc + sub], cw_ref[:, lc:lc + sub], cb_ref[:, lc:lc + sub])
        tail_scr[:, lc:lc + sub] = raw[tm - TAIL_ROWS:, :]
        o_ref[:, c0:c0 + sub] = _silu(y.astype(BF16))

    def ep_gate(raw, c0):
        o_ref[:, c0:c0 + sub] = _sigmoid(raw).astype(BF16)

    def seg_tasks(*pairs):
        return [(fn, c0) for seg, fn in pairs for c0 in range(SEG_OFF[seg], SEG_OFF[seg + 1], sub)]

    tasks = seg_tasks((1, ep_v)) + _interleave(seg_tasks((3, ep_xbc), (0, ep_u)),
                                               seg_tasks((2, ep_z), (4, ep_gate)))

    def produce(k):
        c0 = tasks[k][1]
        if c0 < SEG_OFF[4]:
            return _dot(h_scr[...], w_ref[:, c0:c0 + sub])
        return _dot(h_scr[...], wg_ref[:, c0 - SEG_OFF[4]:c0 - SEG_OFF[4] + sub])

    _staged(len(tasks), produce, lambda k, raw: tasks[k][0](raw, tasks[k][1]),
            stage, jnp.minimum(i, 0))


def _inproj(layer, x, g, w_front, w_gates, w_dt, v_gain, conv_w, conv_b, seq, tm=512):
    m = x.shape[0]
    return pl.pallas_call(
        functools.partial(_inproj_kernel, seq // tm),
        grid=(m // tm,),
        in_specs=[
            pl.BlockSpec((tm, D_MODEL), lambda i: (i, 0)),
            _layer_spec(layer, 1, D_MODEL),
            _layer_spec(layer, D_MODEL, SEG_OFF[4], resident=True),
            _layer_spec(layer, D_MODEL, MAIN_COLS - SEG_OFF[4], resident=True),
            _layer_spec(layer, D_MODEL, LANES, resident=True),
            _layer_spec(layer, 1, GM_WIDTH),
            _layer_spec(layer, SSD_CONV, SSD_XBC),
            _layer_spec(layer, 1, SSD_XBC),
        ],
        out_specs=[
            pl.BlockSpec((tm, MAIN_COLS), lambda i: (i, 0)),
            pl.BlockSpec((tm, LANES), lambda i: (i, 0)),
        ],
        out_shape=[
            jax.ShapeDtypeStruct((m, MAIN_COLS), BF16),
            jax.ShapeDtypeStruct((m, LANES), F32),
        ],
        scratch_shapes=[
            pltpu.VMEM((tm, D_MODEL), BF16),
            pltpu.VMEM((tm, GM_WIDTH), F32),
            pltpu.VMEM((TAIL_ROWS, SSD_XBC), F32),
        ] + [pltpu.VMEM((1, tm, MXU_COLS), F32)] * (STAGE_DEPTH + 1),
        compiler_params=_params("arbitrary"),
        name="inproj",
    )(x, g, w_front, w_gates, w_dt, v_gain, conv_w, conv_b)


def _kvproj_kernel(x_ref, g_ref, w_ref, o_ref):
    h = _rms(x_ref[...], g_ref[...]).astype(BF16)
    o_ref[...] = _dot(h, w_ref[...]).astype(BF16)


def _kvproj(layer, mem, g, w_kv, tm=512):
    m = mem.shape[0]
    n = 2 * D_MODEL
    return pl.pallas_call(
        _kvproj_kernel,
        grid=(m // tm,),
        in_specs=[
            pl.BlockSpec((tm, D_MODEL), lambda i: (i, 0)),
            _layer_spec(layer, 1, D_MODEL),
            _layer_spec(layer, D_MODEL, n),
        ],
        out_specs=pl.BlockSpec((tm, n), lambda i: (i, 0)),
        out_shape=jax.ShapeDtypeStruct((m, n), BF16),
        compiler_params=_params("arbitrary"),
        name="kvproj",
    )(mem, g, w_kv)


def _split2(x):
    hi = x.astype(BF16)
    lo = (x - hi.astype(F32)).astype(BF16)
    return hi, lo


def _split3(x):
    hi = x.astype(BF16)
    r = x - hi.astype(F32)
    mid = r.astype(BF16)
    lo = (r - mid.astype(F32)).astype(BF16)
    return hi, mid, lo


def _mixer_kernel(u_ref, v_ref, z_ref, xs_ref, bc_ref, dt_ref,
                  ws_ref, bexp_ref, dtb_ref, alog_ref, ddiag_ref, e_ref,
                  ya_ref, yb_ref,
                  state_scr, wsm_scr):
    c = pl.program_id(1)
    L = CHUNK
    rowi = lax.broadcasted_iota(jnp.int32, (L, L), 0)
    coli = lax.broadcasted_iota(jnp.int32, (L, L), 1)
    causal = rowi >= coli
    lane = lax.broadcasted_iota(jnp.int32, (L, LANES), 1)
    head_lane = lane < SSD_HEADS
    first_half = lane < SSD_HEAD_DIM
    zero_bf = jnp.zeros((L, LANES), BF16)
    tril = jnp.where(causal, 1.0, 0.0).astype(BF16)
    a_row = -jnp.exp(alog_ref[...])

    @pl.when(c == 0)
    def _():
        state_scr[...] = jnp.zeros_like(state_scr)
        for g in range(GM_GROUPS):
            wsm_scr[g] = jnp.where(causal, ws_ref[g], 0.0).astype(BF16)

    for ck in range(u_ref.shape[0] // L):
        rows = pl.ds(ck * L, L)
        for g in range(GM_GROUPS):
            sl = slice(g * LANES, (g + 1) * LANES)
            mixed = _dot(wsm_scr[g], v_ref[rows, sl]) + bexp_ref[:, sl]
            ya_ref[rows, sl] = u_ref[rows, sl] * mixed.astype(BF16)

        dt = jnp.where(head_lane, _softplus(dt_ref[rows, :] + dtb_ref[...]), 0.0)
        da = dt * a_row
        d_hi, d_mid, d_lo = _split3(da)
        a_cs = _dot(tril, d_hi) + _dot(tril, d_mid) + _dot(tril, d_lo)
        a_cs_t = a_cs.T
        dt_t = dt.T.astype(BF16)
        ea2 = jnp.concatenate(_split2(jnp.exp(a_cs)), axis=1)
        ws2 = jnp.concatenate(_split2(dt * jnp.exp(a_cs[L - 1:L, :] - a_cs)), axis=1)

        for g in range(SSD_GROUPS):
            gsl = slice(g * GROUP_COLS, (g + 1) * GROUP_COLS)
            bm_bf = bc_ref[rows, g * SSD_STATE:(g + 1) * SSD_STATE]
            cm_bf = bc_ref[rows, SSD_GROUPS * SSD_STATE + g * SSD_STATE:
                           SSD_GROUPS * SSD_STATE + (g + 1) * SSD_STATE]
            cb = _dot_nt(cm_bf, bm_bf).astype(BF16)
            xs_g_bf = xs_ref[rows, gsl]
            e_g = e_ref[:, gsl]
            ea_exp = _dot(ea2, e_g)
            wst_exp = _dot(ws2, e_g)
            state_g = state_scr[:, gsl]
            y_g = _dot(cm_bf, state_g.astype(BF16)) * ea_exp
            parts = []
            for pr in range(SSD_HPG // 2):
                sc = []
                for hh in range(2):
                    h = g * SSD_HPG + 2 * pr + hh
                    seg = a_cs[:, h:h + 1] - a_cs_t[h:h + 1, :]
                    lm = jnp.exp(jnp.where(causal, seg, -jnp.inf))
                    sc.append(cb * lm.astype(BF16) * dt_t[h:h + 1, :] + ddiag_ref[h])
                lhs = jnp.concatenate(sc, axis=1)
                x_pair = xs_g_bf[:, pr * LANES:(pr + 1) * LANES]
                rhs = jnp.concatenate([jnp.where(first_half, x_pair, zero_bf),
                                       jnp.where(first_half, zero_bf, x_pair)], axis=0)
                parts.append(_dot(lhs, rhs))
            y_g = y_g + jnp.concatenate(parts, axis=1)
            xw = xs_g_bf * wst_exp.astype(BF16)
            new_state = state_g * ea_exp[L - 1:L, :] + _dot(bm_bf.astype(F32).T.astype(BF16), xw)
            state_scr[:, gsl] = new_state
            y_g = y_g * z_ref[rows, gsl].astype(F32)
            ms = jnp.mean(y_g * y_g, axis=-1, keepdims=True)
            yb_ref[rows, gsl] = (y_g * lax.rsqrt(ms + EPS)).astype(BF16)


def _mixer(layer, proj, dt, w_s, b_exp, dt_bias, a_log, d_diag, e_mat, batch, seq):
    rows = MIX_CHUNKS * CHUNK
    steps = seq // rows
    row = lambda b, c: b * steps + c
    in_specs = [
        pl.BlockSpec((rows, GM_WIDTH), lambda b, c: (row(b, c), 0)),
        pl.BlockSpec((rows, GM_WIDTH), lambda b, c: (row(b, c), 1)),
        pl.BlockSpec((rows, SSD_INNER), lambda b, c: (row(b, c), 1)),
        pl.BlockSpec((rows, SSD_INNER), lambda b, c: (row(b, c), 2)),
        pl.BlockSpec((rows, SSD_BC), lambda b, c: (row(b, c), 6)),
        pl.BlockSpec((rows, LANES), lambda b, c: (row(b, c), 0)),
        _layer_spec(layer, GM_GROUPS, CHUNK, CHUNK),
        _layer_spec(layer, CHUNK, GM_WIDTH),
        _layer_spec(layer, 1, LANES),
        _layer_spec(layer, 1, LANES),
        _layer_spec(layer, SSD_HEADS, CHUNK, CHUNK),
        pl.BlockSpec((2 * LANES, SSD_INNER), lambda b, c: (0, 0)),
    ]
    m = batch * seq
    return pl.pallas_call(
        _mixer_kernel,
        grid=(batch, steps),
        in_specs=in_specs,
        out_specs=[
            pl.BlockSpec((rows, GM_WIDTH), lambda b, c: (row(b, c), 0)),
            pl.BlockSpec((rows, SSD_INNER), lambda b, c: (row(b, c), 0)),
        ],
        out_shape=[
            jax.ShapeDtypeStruct((m, GM_WIDTH), BF16),
            jax.ShapeDtypeStruct((m, SSD_INNER), BF16),
        ],
        scratch_shapes=[
            pltpu.VMEM((SSD_STATE, SSD_INNER), F32),
            pltpu.VMEM((GM_GROUPS, CHUNK, CHUNK), BF16),
        ],
        compiler_params=_params("arbitrary", "arbitrary"),
        name="mixer",
    )(proj, proj, proj, proj, proj, dt, w_s, b_exp, dt_bias, a_log, d_diag, e_mat)


def _merge_kernel(ya_ref, yb_ref, ga_ref, gb_ref, x_ref, wa_ref, wb_ref, wo_ref, g_ref, o_ref):
    for r0 in range(0, x_ref.shape[0], ROW_SUB):
        rows = pl.ds(r0, ROW_SUB)
        ya = _dot(ya_ref[rows, :], wa_ref[...])
        yb = _dot(yb_ref[rows, :], wb_ref[...])
        mixed = ga_ref[rows, :].astype(F32) * ya + gb_ref[rows, :].astype(F32) * yb
        y = _dot(mixed.astype(BF16), wo_ref[...])
        o_ref[rows, :] = x_ref[rows, :] + _rms(y, g_ref[...])


def _merge(layer, ya, yb, proj, x, w_a, w_b, w_o, g, tm=1024):
    m = x.shape[0]
    return pl.pallas_call(
        _merge_kernel,
        grid=(m // tm,),
        in_specs=[
            pl.BlockSpec((tm, GM_WIDTH), lambda i: (i, 0)),
            pl.BlockSpec((tm, SSD_INNER), lambda i: (i, 0)),
            pl.BlockSpec((tm, D_MODEL), lambda i: (i, 7)),
            pl.BlockSpec((tm, D_MODEL), lambda i: (i, 8)),
            pl.BlockSpec((tm, D_MODEL), lambda i: (i, 0)),
            _layer_spec(layer, GM_WIDTH, D_MODEL, resident=True),
            _layer_spec(layer, SSD_INNER, D_MODEL, resident=True),
            _layer_spec(layer, D_MODEL, D_MODEL, resident=True),
            _layer_spec(layer, 1, D_MODEL),
        ],
        out_specs=pl.BlockSpec((tm, D_MODEL), lambda i: (i, 0)),
        out_shape=jax.ShapeDtypeStruct((m, D_MODEL), F32),
        compiler_params=_params("arbitrary"),
        name="merge",
    )(ya, yb, proj, proj, x, w_a, w_b, w_o, g)


def _xattn_kernel(x_ref, gpre_ref, wq_ref, k_ref, v_ref, wo_ref, gpost_ref, o_ref):
    for r0 in range(0, x_ref.shape[0], ROW_SUB):
        rows = pl.ds(r0, ROW_SUB)
        x = x_ref[rows, :]
        h = _rms(x, gpre_ref[...]).astype(BF16)
        q = (_dot(h, wq_ref[...]) * (X_HEAD_DIM ** -0.5)).astype(BF16)
        outs = []
        for hd in range(X_HEADS):
            sl = slice(hd * X_HEAD_DIM, (hd + 1) * X_HEAD_DIM)
            s = _dot_nt(q[:, sl], k_ref[:, sl])
            e = jnp.exp(s - jnp.max(s, axis=-1, keepdims=True))
            p = e / jnp.sum(e, axis=-1, keepdims=True)
            outs.append(_dot(p.astype(BF16), v_ref[:, sl]).astype(BF16))
        o = jnp.concatenate(outs, axis=1)
        y = _dot(o, wo_ref[...])
        o_ref[rows, :] = x + _rms(y, gpost_ref[...])


def _xattn(layer, x, kv, g_pre, w_q, w_o, g_post, seq, tm=2048):
    m = x.shape[0]
    tiles = seq // tm
    return pl.pallas_call(
        _xattn_kernel,
        grid=(m // tm,),
        in_specs=[
            pl.BlockSpec((tm, D_MODEL), lambda i: (i, 0)),
            _layer_spec(layer, 1, D_MODEL),
            _layer_spec(layer, D_MODEL, D_MODEL),
            pl.BlockSpec((MEM_LEN, D_MODEL), lambda i: (i // tiles, 0)),
            pl.BlockSpec((MEM_LEN, D_MODEL), lambda i: (i // tiles, 1)),
            _layer_spec(layer, D_MODEL, D_MODEL),
            _layer_spec(layer, 1, D_MODEL),
        ],
        out_specs=pl.BlockSpec((tm, D_MODEL), lambda i: (i, 0)),
        out_shape=jax.ShapeDtypeStruct((m, D_MODEL), F32),
        compiler_params=_params("arbitrary"),
        name="xattn",
    )(x, g_pre, w_q, kv, kv, w_o, g_post)


def _ffn_kernel(tiles_per_seq, x_ref, gpre_ref, wup_ref, cw_ref, cb_ref, wd_ref, gpost_ref, o_ref,
                h_scr, g_scr, tail_scr, *stage):
    i = pl.program_id(0)
    tm = x_ref.shape[0]
    sub = MXU_COLS
    h_scr[...] = _rms(x_ref[...], gpre_ref[...]).astype(BF16)

    @pl.when(i % tiles_per_seq == 0)
    def _():
        tail_scr[...] = jnp.zeros_like(tail_scr)

    def produce(k):
        c0 = (k // 2) * sub + (k % 2) * FFN_DIM
        return _dot(h_scr[...], wup_ref[:, c0:c0 + sub])

    pending = {}
    down = []
    n_sub = FFN_DIM // sub

    def consume(k, raw):
        c0 = (k // 2) * sub + (k % 2) * FFN_DIM
        y = _causal_conv(raw, tail_scr[:, c0:c0 + sub], cw_ref[:, c0:c0 + sub], cb_ref[:, c0:c0 + sub])
        tail_scr[:, c0:c0 + sub] = raw[tm - TAIL_ROWS:, :]
        if k % 2 == 0:
            pending[k // 2] = _gelu(y)
            return
        j = k // 2
        g_scr[:, j * sub:(j + 1) * sub] = (pending.pop(j) * y).astype(BF16)
        if (j + 1) % FFN_DOWN_GROUP == 0 or j == n_sub - 1:
            lo = (j // FFN_DOWN_GROUP) * FFN_DOWN_GROUP * sub
            down.append(_dot(g_scr[:, lo:(j + 1) * sub], wd_ref[lo:(j + 1) * sub, :]))

    _staged(2 * n_sub, produce, consume, stage, jnp.minimum(i, 0))

    o_ref[...] = x_ref[...] + _rms(sum(down), gpost_ref[...])


def _ffn(layer, x, g_pre, w_up, conv_w, conv_b, w_down, g_post, seq, tm=512):
    m = x.shape[0]
    return pl.pallas_call(
        functools.partial(_ffn_kernel, seq // tm),
        grid=(m // tm,),
        in_specs=[
            pl.BlockSpec((tm, D_MODEL), lambda i: (i, 0)),
            _layer_spec(layer, 1, D_MODEL),
            _layer_spec(layer, D_MODEL, 2 * FFN_DIM, resident=True),
            _layer_spec(layer, FFN_CONV, 2 * FFN_DIM),
            _layer_spec(layer, 1, 2 * FFN_DIM),
            _layer_spec(layer, FFN_DIM, D_MODEL, resident=True),
            _layer_spec(layer, 1, D_MODEL),
        ],
        out_specs=pl.BlockSpec((tm, D_MODEL), lambda i: (i, 0)),
        out_shape=jax.ShapeDtypeStruct((m, D_MODEL), F32),
        scratch_shapes=[
            pltpu.VMEM((tm, D_MODEL), BF16),
            pltpu.VMEM((tm, FFN_DIM), BF16),
            pltpu.VMEM((TAIL_ROWS, 2 * FFN_DIM), F32),
        ] + [pltpu.VMEM((1, tm, MXU_COLS), F32)] * (FFN_STAGE_DEPTH + 1),
        compiler_params=_params("arbitrary"),
        name="ffn",
    )(x, g_pre, w_up, conv_w, conv_b, w_down, g_post)


def _rows(v):
    return v.reshape(v.shape[0], 1, v.shape[1])


def kernel(x, mem, norm_pre_mix, norm_post_mix, norm_pre_mem, norm_mem_kv, norm_post_mem,
           norm_pre_ffn, norm_post_ffn, w_in, gm_v_norm, gm_w_s, gm_b_s, ssd_conv_w, ssd_conv_b,
           ssd_dt_bias, ssd_a_log, ssd_d, ssd_norm, w_branch_a, w_branch_b, w_out,
           xa_w_q, xa_w_kv, xa_w_o, ffn_w_up, ffn_conv_w, ffn_conv_b, ffn_w_down):
    batch, seq, _ = x.shape
    depth = w_in.shape[0]
    xf = x.reshape(batch * seq, D_MODEL)
    memf = mem.reshape(batch * MEM_LEN, D_MODEL)

    w_front = w_in.astype(BF16)
    w_gates = w_front[:, :, DT_OFF + SSD_HEADS:]
    w_dt = jnp.pad(w_front[:, :, DT_OFF:DT_OFF + SSD_HEADS], ((0, 0), (0, 0), (0, LANES - SSD_HEADS)))
    pad_heads = lambda v: _rows(jnp.pad(v, ((0, 0), (0, LANES - SSD_HEADS))))
    b_exp = jnp.repeat(jnp.swapaxes(gm_b_s, 1, 2), LANES, axis=2)
    d_diag = (ssd_d[:, :, None, None] * jnp.eye(CHUNK, dtype=F32)).astype(BF16)
    e_mat = (lax.broadcasted_iota(jnp.int32, (2 * LANES, SSD_INNER), 0) % LANES
             == lax.broadcasted_iota(jnp.int32, (2 * LANES, SSD_INNER), 1) // SSD_HEAD_DIM).astype(BF16)
    bf = lambda w: w.astype(BF16)
    w_a, w_b, w_o = bf(w_branch_a), bf(w_branch_b * ssd_norm[:, :, None]), bf(w_out)
    w_q, w_kv, w_xo = bf(xa_w_q), bf(xa_w_kv), bf(xa_w_o)
    w_up, w_down = bf(ffn_w_up), bf(ffn_w_down)

    for l in range(depth):
        proj, dt = _inproj(l, xf, _rows(norm_pre_mix), w_front, w_gates, w_dt, _rows(gm_v_norm),
                           ssd_conv_w, _rows(ssd_conv_b), seq)
        ya, yb = _mixer(l, proj, dt, gm_w_s, b_exp, pad_heads(ssd_dt_bias), pad_heads(ssd_a_log),
                        d_diag, e_mat, batch, seq)
        xf = _merge(l, ya, yb, proj, xf, w_a, w_b, w_o, _rows(norm_post_mix))

        kv = _kvproj(l, memf, _rows(norm_mem_kv), w_kv)
        xf = _xattn(l, xf, kv, _rows(norm_pre_mem), w_q, w_xo, _rows(norm_post_mem), seq)

        xf = _ffn(l, xf, _rows(norm_pre_ffn), w_up, ffn_conv_w, _rows(ffn_conv_b), w_down,
                  _rows(norm_post_ffn), seq)
    return xf.reshape(batch, seq, D_MODEL)
```
